```python
import jax, jax.numpy as jnp
from jax import lax
import numpy as np

D_MODEL = 1024
BATCH = 16
SEQ = 256
DEPTH = 4
DEC_BATCH = 8
DEC_SEQ = 4096
PAST_LEN = 512

GRID_W = 64
MIX_W = D_MODEL
M_WIDTH = MIX_W // 2
C_WIDTH = MIX_W - M_WIDTH
M_HEADS = 4
M_HEAD_DIM = M_WIDTH // M_HEADS
CHUNK = 128
CONV_W = 31
N_GATES = 4 * M_HEADS
PROJ_W = 4 * M_WIDTH + 2 * C_WIDTH + N_GATES
D_FF = 2816
N_EXPERTS = 8
TOP_K = 2
D_FF_EXPERT = 3584
N_DENSE = (DEPTH + 1) // 2
N_MOE = DEPTH // 2
EPS = 1e-6
NEG = -1e30
F32 = jnp.float32

kernel_name = "hybrid_mlstm_conformer_diffusion_step"


def rmsnorm(x, g):
    xf = x.astype(F32)
    y = xf * lax.rsqrt(jnp.mean(xf * xf, axis=-1, keepdims=True) + EPS)
    return (y * g.astype(F32)).astype(x.dtype)


def layernorm(x, g, b):
    xf = x.astype(F32)
    mu = jnp.mean(xf, axis=-1, keepdims=True)
    var = jnp.mean(jnp.square(xf - mu), axis=-1, keepdims=True)
    y = (xf - mu) * lax.rsqrt(var + EPS)
    return (y * g.astype(F32) + b.astype(F32)).astype(x.dtype)


def grid_sincos(T, dtype):
    rows = T // GRID_W
    row = jnp.repeat(jnp.arange(rows, dtype=F32), GRID_W)
    col = jnp.tile(jnp.arange(GRID_W, dtype=F32), rows)
    quarter = D_MODEL // 4
    omega = 1.0 / (10000.0 ** (jnp.arange(quarter, dtype=F32) / quarter))

    def emb(p):
        a = p[:, None] * omega[None, :]
        return jnp.concatenate([jnp.sin(a), jnp.cos(a)], axis=-1)

    return jnp.concatenate([emb(row), emb(col)], axis=-1).astype(dtype)


def mlstm_chunkwise(q, k, v, log_i, log_f, C0, n0, m0):
    B, H, T, Dh = q.shape
    nc = T // CHUNK

    def to_chunks(a):
        a = a.reshape(a.shape[:2] + (nc, CHUNK) + a.shape[3:])
        return jnp.moveaxis(a, 2, 0)

    tril = jnp.tril(jnp.ones((CHUNK, CHUNK), dtype=bool))

    def step(carry, inp):
        C, n, m = carry
        qc, kc, vc, ic, fc = inp
        b = jnp.cumsum(fc, axis=-1)
        dmat = jnp.where(tril, b[..., :, None] - b[..., None, :] + ic[..., None, :], NEG)
        inter = b + m[..., None]
        m_t = jnp.maximum(inter, jnp.max(dmat, axis=-1))
        s = jnp.einsum("bhtd,bhsd->bhts", qc, kc) * jnp.exp(dmat - m_t[..., None])
        w_inter = jnp.exp(inter - m_t)
        num = jnp.einsum("bhts,bhsv->bhtv", s, vc) + w_inter[..., None] * jnp.einsum("bhtk,bhkv->bhtv", qc, C)
        qn = jnp.sum(s, axis=-1) + w_inter * jnp.einsum("bhtk,bhk->bht", qc, n)
        h = num / jnp.maximum(jnp.abs(qn), jnp.exp(-m_t))[..., None]
        b_last = b[..., -1]
        w_log = b_last[..., None] - b + ic
        m_new = jnp.maximum(b_last + m, jnp.max(w_log, axis=-1))
        decay = jnp.exp(b_last + m - m_new)
        w_s = jnp.exp(w_log - m_new[..., None])
        C_new = decay[..., None, None] * C + jnp.einsum("bhs,bhsk,bhsv->bhkv", w_s, kc, vc)
        n_new = decay[..., None] * n + jnp.einsum("bhs,bhsk->bhk", w_s, kc)
        return (C_new, n_new, m_new), h

    (C, n, m), hs = lax.scan(step, (C0, n0, m0), tuple(to_chunks(a) for a in (q, k, v, log_i, log_f)))
    h = jnp.moveaxis(hs, 0, 2).reshape(B, H, T, Dh)
    return h, (C, n, m)


def mixer(h, w_in, b_gate, g_mh, w_dw, b_dw, ln_g, ln_b, w_out, init_f, init_b):
    B, T, _ = h.shape
    proj = h @ w_in
    q, k, v, o, ga, gb, gates = jnp.split(
        proj, [M_WIDTH, 2 * M_WIDTH, 3 * M_WIDTH, 4 * M_WIDTH, 4 * M_WIDTH + C_WIDTH, 4 * M_WIDTH + 2 * C_WIDTH], axis=-1)

    def heads(a):
        return a.reshape(B, T, M_HEADS, M_HEAD_DIM).transpose(0, 2, 1, 3).astype(F32)

    q, k, v = heads(q), heads(k) * (M_HEAD_DIM ** -0.5), heads(v)
    gates = (gates + b_gate).astype(F32).reshape(B, T, 2, 2, M_HEADS).transpose(2, 3, 0, 4, 1)
    log_i = gates[:, 0]
    log_f = jax.nn.log_sigmoid(gates[:, 1])
    h_f, st_f = mlstm_chunkwise(q, k, v, log_i[0], log_f[0], *init_f)

    def flip(a):
        return jnp.flip(a, axis=2)

    h_b, st_b = mlstm_chunkwise(flip(q), flip(k), flip(v), flip(log_i[1]), flip(log_f[1]), *init_b)
    hm = h_f + flip(h_b)
    mu = jnp.mean(hm, axis=-1, keepdims=True)
    var = jnp.mean(jnp.square(hm - mu), axis=-1, keepdims=True)
    hm = ((hm - mu) * lax.rsqrt(var + EPS)).transpose(0, 2, 1, 3).reshape(B, T, M_WIDTH) * g_mh.astype(F32)
    hm = (jax.nn.sigmoid(o.astype(F32)) * hm).astype(h.dtype)

    u = ga * jax.nn.sigmoid(gb)
    u = lax.conv_general_dilated(
        u, w_dw.astype(u.dtype)[:, None, :], window_strides=(1,), padding=[(CONV_W // 2, CONV_W // 2)],
        dimension_numbers=("NWC", "WIO", "NWC"), feature_group_count=C_WIDTH) + b_dw
    u = jax.nn.silu(layernorm(u, ln_g, ln_b))

    out = jnp.concatenate([hm, u], axis=-1) @ w_out
    return out, st_f, st_b


def swiglu(h, wg, wu, wd):
    return (jax.nn.silu(h @ wg) * (h @ wu)) @ wd


def moe_swiglu(h, w_router, wg, wu, wd):
    B, T, Dm = h.shape
    hf = h.reshape(B * T, Dm)
    logits = (hf @ w_router).astype(F32)
    top_v, top_i = lax.top_k(logits, TOP_K)
    top_w = jax.nn.softmax(top_v, axis=-1)
    gate = jnp.sum(jax.nn.one_hot(top_i, N_EXPERTS, dtype=F32) * top_w[..., None], axis=1)
    y = jnp.zeros_like(hf)
    for e in range(N_EXPERTS):
        y = y + gate[:, e:e + 1].astype(hf.dtype) * swiglu(hf, wg[e], wu[e], wd[e])
    return y.reshape(B, T, Dm)


def trunk(x, cond, cache_C, cache_n, cache_m, w_ada, b_ada, g_mix, w_in, b_gate, g_mh, w_dw, b_dw, ln_g, ln_b,
          w_out, g_ffn, w_ff_gate, w_ff_up, w_ff_down, w_router, w_moe_gate, w_moe_up, w_moe_down, g_final):
    B = x.shape[0]
    states = []
    for l in range(DEPTH):
        mod = (jax.nn.silu(cond) @ w_ada[l] + b_ada[l])[:, None, :]
        sh1, sc1, g1, sh2, sc2, g2 = jnp.split(mod, 6, axis=-1)
        if cache_C is None:
            zero = (jnp.zeros((B, M_HEADS, M_HEAD_DIM, M_HEAD_DIM), F32), jnp.zeros((B, M_HEADS, M_HEAD_DIM), F32),
                    jnp.full((B, M_HEADS), NEG, F32))
            init_f, init_b = zero, zero
        else:
            init_f = (cache_C[:, l, 0].astype(F32), cache_n[:, l, 0].astype(F32), cache_m[:, l, 0].astype(F32))
            init_b = (cache_C[:, l, 1].astype(F32), cache_n[:, l, 1].astype(F32), cache_m[:, l, 1].astype(F32))
        h = rmsnorm(x, g_mix[l]) * (1 + sc1) + sh1
        out, st_f, st_b = mixer(h, w_in[l], b_gate[l], g_mh[l], w_dw[l], b_dw[l], ln_g[l], ln_b[l], w_out[l],
                                init_f, init_b)
        x = x + g1 * out
        h = rmsnorm(x, g_ffn[l]) * (1 + sc2) + sh2
        if l % 2 == 0:
            f = swiglu(h, w_ff_gate[l // 2], w_ff_up[l // 2], w_ff_down[l // 2])
        else:
            f = moe_swiglu(h, w_router[l // 2], w_moe_gate[l // 2], w_moe_up[l // 2], w_moe_down[l // 2])
        x = x + g2 * f
        states.append((st_f, st_b))
    return rmsnorm(x, g_final), states


def setup_inputs(seed: int = 0) -> dict:
    key = jax.random.key(seed)
    ks = jax.random.split(key, 32)

    def nrm(k, shape, s):
        return jax.random.normal(k, shape, F32) * s

    Dm = D_MODEL
    i_bias = nrm(ks[11], (DEPTH, 2, M_HEADS), 0.1)
    f_bias = jnp.linspace(3.0, 6.0, M_HEADS, dtype=F32)[None, None, :] + nrm(ks[12], (DEPTH, 2, M_HEADS), 0.1)
    b_gate = jnp.stack([i_bias, f_bias], axis=2).reshape(DEPTH, N_GATES)
    return {
        "x_prompt": nrm(ks[0], (BATCH, SEQ, Dm), 1.0),
        "x_sample": nrm(ks[1], (DEC_BATCH, DEC_SEQ, Dm), 1.0),
        "state_C": nrm(ks[2], (DEC_BATCH, DEPTH, 2, M_HEADS, M_HEAD_DIM, M_HEAD_DIM), 0.3),
        "state_n": nrm(ks[3], (DEC_BATCH, DEPTH, 2, M_HEADS, M_HEAD_DIM), 0.3),
        "state_m": nrm(ks[4], (DEC_BATCH, DEPTH, 2, M_HEADS), 0.5),
        "c": nrm(ks[5], (DEC_BATCH, Dm), 1.0),
        "c_ctx": nrm(ks[6], (Dm,), 1.0),
        "w_ada": nrm(ks[7], (DEPTH, Dm, 6 * Dm), 0.5 * Dm ** -0.5),
        "b_ada": nrm(ks[8], (DEPTH, 6 * Dm), 0.02),
        "g_mix": 1.0 + nrm(ks[9], (DEPTH, Dm), 0.05),
        "w_in": nrm(ks[10], (DEPTH, Dm, PROJ_W), Dm ** -0.5),
        "b_gate": b_gate,
        "g_mh": 1.0 + nrm(ks[13], (DEPTH, M_WIDTH), 0.05),
        "w_dw": nrm(ks[14], (DEPTH, CONV_W, C_WIDTH), CONV_W ** -0.5),
        "b_dw": nrm(ks[15], (DEPTH, C_WIDTH), 0.02),
        "ln_g": 1.0 + nrm(ks[16], (DEPTH, C_WIDTH), 0.05),
        "ln_b": nrm(ks[17], (DEPTH, C_WIDTH), 0.02),
        "w_out": nrm(ks[18], (DEPTH, MIX_W, Dm), MIX_W ** -0.5),
        "g_ffn": 1.0 + nrm(ks[19], (DEPTH, Dm), 0.05),
        "w_ff_gate": nrm(ks[20], (N_DENSE, Dm, D_FF), Dm ** -0.5),
        "w_ff_up": nrm(ks[21], (N_DENSE, Dm, D_FF), Dm ** -0.5),
        "w_ff_down": nrm(ks[22], (N_DENSE, D_FF, Dm), D_FF ** -0.5),
        "w_router": nrm(ks[23], (N_MOE, Dm, N_EXPERTS), Dm ** -0.5),
        "w_moe_gate": nrm(ks[24], (N_MOE, N_EXPERTS, Dm, D_FF_EXPERT), Dm ** -0.5),
        "w_moe_up": nrm(ks[25], (N_MOE, N_EXPERTS, Dm, D_FF_EXPERT), Dm ** -0.5),
        "w_moe_down": nrm(ks[26], (N_MOE, N_EXPERTS, D_FF_EXPERT, Dm), D_FF_EXPERT ** -0.5),
        "g_final": 1.0 + nrm(ks[27], (Dm,), 0.05),
    }


def reference(x_prompt, x_sample, state_C, state_n, state_m, c, c_ctx, w_ada, b_ada, g_mix, w_in, b_gate, g_mh,
              w_dw, b_dw, ln_g, ln_b, w_out, g_ffn, w_ff_gate, w_ff_up, w_ff_down, w_router, w_moe_gate, w_moe_up,
              w_moe_down, g_final):
    weights = (w_ada, b_ada, g_mix, w_in, b_gate, g_mh, w_dw, b_dw, ln_g, ln_b, w_out, g_ffn, w_ff_gate, w_ff_up,
               w_ff_down, w_router, w_moe_gate, w_moe_up, w_moe_down, g_final)
    y_prompt, ctx_states = trunk(x_prompt, c_ctx[None, :], None, None, None, *weights)
    new_state_C = jnp.stack([jnp.stack([sf[0], sb[0]], axis=1) for sf, sb in ctx_states], axis=1)
    new_state_n = jnp.stack([jnp.stack([sf[1], sb[1]], axis=1) for sf, sb in ctx_states], axis=1)
    new_state_m = jnp.stack([jnp.stack([sf[2], sb[2]], axis=1) for sf, sb in ctx_states], axis=1)
    xs = x_sample + grid_sincos(x_sample.shape[1], x_sample.dtype)[None]
    y_sample, _ = trunk(xs, c, state_C, state_n, state_m, *weights)
    return (y_prompt, y_sample, new_state_C, new_state_n, new_state_m)
```

```python
import functools

import jax
import jax.numpy as jnp
from jax import lax
from jax.experimental import pallas as pl
from jax.experimental.pallas import tpu as pltpu

F32 = jnp.float32
BF16 = jnp.bfloat16
EPS = 1e-6
NEG = -1e30

M_HEADS = 4
HEAD_DIM = 128
CHUNK = 128
CONV_W = 31
CONV_HALO = 16
N_EXPERTS = 8
LANES = 128
VMEM_LIMIT = 56 * 1024 * 1024


def _cparams(sem):
    return pltpu.CompilerParams(dimension_semantics=sem, vmem_limit_bytes=VMEM_LIMIT)


def _silu(a):
    return a * jax.nn.sigmoid(a)


def _rms_mod(x, g, scale, shift):
    y = x * lax.rsqrt(jnp.mean(x * x, axis=-1, keepdims=True) + EPS) * g
    return y * (1.0 + scale) + shift


def _ada_kernel(c_ref, w_ref, b_ref, o_ref):
    s = _silu(c_ref[...]).astype(BF16)
    o_ref[...] = jnp.dot(s, w_ref[...].astype(BF16), preferred_element_type=F32) + b_ref[...]


def _ada(cond, w_ada, b_ada):
    R, D = cond.shape
    L, _, D6 = w_ada.shape
    tn = 1536
    return pl.pallas_call(
        _ada_kernel,
        grid=(L, D6 // tn),
        in_specs=[
            pl.BlockSpec((R, D), lambda l, j: (0, 0)),
            pl.BlockSpec((None, D, tn), lambda l, j: (l, 0, j)),
            pl.BlockSpec((None, 1, tn), lambda l, j: (l, 0, j)),
        ],
        out_specs=pl.BlockSpec((None, R, tn), lambda l, j: (l, 0, j)),
        out_shape=jax.ShapeDtypeStruct((L, R, D6), F32),
        compiler_params=_cparams(("arbitrary", "arbitrary")),
        name="ada",
    )(cond, w_ada, b_ada.reshape(L, 1, D6))


def _inproj_kernel(x_ref, mod_ref, g_ref, w_ref, wg_ref, bg_ref, proj_ref, gates_ref, *, d, k_scale):
    h = _rms_mod(x_ref[...], g_ref[...], mod_ref[:, d:2 * d], mod_ref[:, 0:d]).astype(BF16)
    nw = w_ref.shape[1]
    cw = 512
    for j in range(nw // cw):
        r = jnp.dot(h, w_ref[:, j * cw:(j + 1) * cw], preferred_element_type=F32)
        if j == 1:
            r = r * k_scale
        proj_ref[:, j * cw:(j + 1) * cw] = r.astype(BF16)
    gates_ref[...] = jnp.dot(h, wg_ref[...], preferred_element_type=F32) + bg_ref[...]


def _inproj(x, mod, g, w_main, w_gate, b_gate, tm, row_of_tile):
    N, D = x.shape
    nw = w_main.shape[1]
    return pl.pallas_call(
        functools.partial(_inproj_kernel, d=D, k_scale=HEAD_DIM ** -0.5),
        grid=(N // tm,),
        in_specs=[
            pl.BlockSpec((tm, D), lambda i: (i, 0)),
            pl.BlockSpec((None, 1, 6 * D), lambda i: (row_of_tile(i, tm), 0, 0)),
            pl.BlockSpec((1, D), lambda i: (0, 0)),
            pl.BlockSpec((D, nw), lambda i: (0, 0)),
            pl.BlockSpec((D, LANES), lambda i: (0, 0)),
            pl.BlockSpec((1, LANES), lambda i: (0, 0)),
        ],
        out_specs=[
            pl.BlockSpec((tm, nw), lambda i: (i, 0)),
            pl.BlockSpec((tm, LANES), lambda i: (i, 0)),
        ],
        out_shape=[
            jax.ShapeDtypeStruct((N, nw), BF16),
            jax.ShapeDtypeStruct((N, LANES), F32),
        ],
        compiler_params=_cparams(("arbitrary",)),
        name="inproj",
    )(x, mod, g, w_main, w_gate, b_gate)


def _mlstm_step(qc, kc, vc, i_row, b_row, i_col, b_col, btot, mask, c_ref, n, m):
    c_mat = c_ref[...]
    dmat = jnp.where(mask, b_col - b_row + i_row, NEG)
    inter = b_col + m
    m_t = jnp.maximum(inter, jnp.max(dmat, axis=1, keepdims=True))
    s = lax.dot_general(qc, kc, (((1,), (1,)), ((), ())), preferred_element_type=F32) * jnp.exp(dmat - m_t)
    w_inter = jnp.exp(inter - m_t)
    num = (jnp.dot(s.astype(BF16), vc, preferred_element_type=F32)
           + w_inter * jnp.dot(qc, c_mat.astype(BF16), preferred_element_type=F32))
    qn = (jnp.sum(s, axis=1, keepdims=True)
          + w_inter * jnp.sum(qc.astype(F32) * n, axis=1, keepdims=True))
    h = num / jnp.maximum(jnp.abs(qn), jnp.exp(-m_t))
    w_log = btot - b_row + i_row
    m_new = jnp.maximum(btot + m, jnp.max(w_log, axis=1, keepdims=True))
    decay = jnp.exp(btot + m - m_new)
    kw = kc.astype(F32) * jnp.exp(btot - b_col + i_col - m_new)
    c_ref[...] = decay * c_mat + lax.dot_general(kw.astype(BF16), vc, (((0,), (0,)), ((), ())),
                                                 preferred_element_type=F32)
    n_new = decay * n + jnp.sum(kw, axis=0, keepdims=True)
    return h, n_new, m_new


def _mlstm_kernel(q_ref, k_ref, v_ref, o_ref, gr_ref, gmh_ref, c0_ref, n0_ref, m0_ref,
                  hm_ref, ct_ref, nt_ref, mt_ref, hf_ref, hb_ref, *, nc):
    L = CHUNK
    row = lax.broadcasted_iota(jnp.int32, (L, L), 0)
    col = lax.broadcasted_iota(jnp.int32, (L, L), 1)
    lower = col <= row
    upper = col >= row
    cum_f = upper.astype(F32)
    cum_b = lower.astype(F32)
    rid = lax.broadcasted_iota(jnp.int32, (8, L), 0)
    zpad = jnp.zeros((L - 8, L), F32)

    ct_ref[...] = c0_ref[...]

    def gate_tiles(c):
        g = gr_ref[:, pl.ds(pl.multiple_of(c * L, L), L)]
        lf = jax.nn.log_sigmoid(g)
        bf = jnp.dot(lf, cum_f, preferred_element_type=F32, precision=lax.Precision.HIGHEST)
        bb = jnp.dot(lf, cum_b, preferred_element_type=F32, precision=lax.Precision.HIGHEST)
        r = jnp.where(rid == 1, bf, jnp.where(rid == 3, bb, g))
        tot = jnp.sum(jnp.where((rid == 1) | (rid == 3), lf, 0.0), axis=1, keepdims=True)
        rt = jnp.concatenate([r, zpad], axis=0).T
        return r, rt, tot

    def body(c, carry):
        nf, mf, nb, mb = carry
        cf = c
        cb = nc - 1 - c
        r, rt, tot = gate_tiles(cf)
        sl = pl.ds(pl.multiple_of(cf * L, L), L)
        h, nf, mf = _mlstm_step(q_ref[sl, :], k_ref[sl, :], v_ref[sl, :],
                                r[0:1, :], r[1:2, :], rt[:, 0:1], rt[:, 1:2], tot[1:2, :],
                                lower, ct_ref.at[0], nf, mf)
        hf_ref[sl, :] = h
        r, rt, tot = gate_tiles(cb)
        sl = pl.ds(pl.multiple_of(cb * L, L), L)
        h, nb, mb = _mlstm_step(q_ref[sl, :], k_ref[sl, :], v_ref[sl, :],
                                r[2:3, :], r[3:4, :], rt[:, 2:3], rt[:, 3:4], tot[3:4, :],
                                upper, ct_ref.at[1], nb, mb)
        hb_ref[sl, :] = h
        return nf, mf, nb, mb

    init = (n0_ref[0], m0_ref[0][:, 0:1], n0_ref[1], m0_ref[1][:, 0:1])
    nf, mf, nb, mb = lax.fori_loop(0, nc, body, init)
    nt_ref[0] = nf
    nt_ref[1] = nb
    mt_ref[0] = jnp.broadcast_to(mf, (1, LANES))
    mt_ref[1] = jnp.broadcast_to(mb, (1, LANES))

    gmh = gmh_ref[...]

    def fin(c, _):
        sl = pl.ds(pl.multiple_of(c * L, L), L)
        hm = hf_ref[sl, :] + hb_ref[sl, :]
        mu = jnp.mean(hm, axis=-1, keepdims=True)
        var = jnp.mean(jnp.square(hm - mu), axis=-1, keepdims=True)
        y = (hm - mu) * lax.rsqrt(var + EPS) * gmh
        hm_ref[sl, :] = (jax.nn.sigmoid(o_ref[sl, :].astype(F32)) * y).astype(BF16)
        return 0

    lax.fori_loop(0, nc, fin, 0)


def _mlstm(proj, gr, g_mh, c0, n0, m0, hm_prev, *, n_seq, t, row0, n_tokens):
    H, Dh = M_HEADS, HEAD_DIM
    b0 = row0 // t
    assert row0 % t == 0 and t % CHUNK == 0
    nc = t // CHUNK

    def pspec(part):
        return pl.BlockSpec((t, Dh), lambda b, h: (b0 + b, part * H + h))

    st_spec_c = pl.BlockSpec((None, 2, None, Dh, Dh), lambda b, h: (b, 0, h, 0, 0))
    st_spec_v = pl.BlockSpec((None, 2, None, 1, LANES), lambda b, h: (b, 0, h, 0, 0))
    in_specs = [pspec(0), pspec(1), pspec(2), pspec(3),
                pl.BlockSpec((None, 8, t), lambda b, h: (h, 0, b0 + b)),
                pl.BlockSpec((1, Dh), lambda b, h: (0, h)),
                st_spec_c, st_spec_v, st_spec_v]
    args = [proj, proj, proj, proj, gr, g_mh, c0, n0, m0]
    aliases = {}
    if hm_prev is not None:
        in_specs.append(pl.BlockSpec(memory_space=pl.ANY))
        args.append(hm_prev)
        aliases = {len(args) - 1: 0}

    def kern(*refs):
        if hm_prev is not None:
            refs = refs[:9] + refs[10:]
        _mlstm_kernel(*refs, nc=nc)

    return pl.pallas_call(
        kern,
        grid=(n_seq, H),
        in_specs=in_specs,
        out_specs=[
            pl.BlockSpec((t, Dh), lambda b, h: (b0 + b, h)),
            st_spec_c, st_spec_v, st_spec_v,
        ],
        out_shape=[
            jax.ShapeDtypeStruct((n_tokens, H * Dh), BF16),
            jax.ShapeDtypeStruct((n_seq, 2, H, Dh, Dh), F32),
            jax.ShapeDtypeStruct((n_seq, 2, H, 1, LANES), F32),
            jax.ShapeDtypeStruct((n_seq, 2, H, 1, LANES), F32),
        ],
        scratch_shapes=[pltpu.VMEM((t, Dh), F32), pltpu.VMEM((t, Dh), F32)],
        input_output_aliases=aliases,
        compiler_params=_cparams(("arbitrary", "arbitrary")),
        name="mlstm",
    )(*args)


def _conv_kernel(ga_ref, gb_ref, w_ref, bdw_ref, lng_ref, lnb_ref, u_ref, pad_ref, *, t, tt):
    cw = pad_ref.shape[1]
    zeros = jnp.zeros((CONV_HALO, cw), F32)
    pad_ref[0:CONV_HALO, :] = zeros
    pad_ref[CONV_HALO + t:2 * CONV_HALO + t, :] = zeros

    def fill(i, _):
        sl = pl.ds(pl.multiple_of(i * tt, tt), tt)
        glu = ga_ref[sl, :].astype(F32) * jax.nn.sigmoid(gb_ref[sl, :].astype(F32))
        pad_ref[pl.ds(pl.multiple_of(CONV_HALO + i * tt, 8), tt), :] = glu
        return 0

    lax.fori_loop(0, t // tt, fill, 0)
    w = w_ref[...]
    shift0 = CONV_HALO - CONV_W // 2

    def conv(i, _):
        t0 = pl.multiple_of(i * tt, tt)
        win = pad_ref[pl.ds(t0, tt + 2 * CONV_HALO), :]
        acc = jnp.broadcast_to(bdw_ref[...], (tt, cw))
        for j in range(CONV_W):
            acc = acc + win[shift0 + j:shift0 + j + tt, :] * w[j:j + 1, :]
        mu = jnp.mean(acc, axis=-1, keepdims=True)
        var = jnp.mean(jnp.square(acc - mu), axis=-1, keepdims=True)
        y = (acc - mu) * lax.rsqrt(var + EPS) * lng_ref[...] + lnb_ref[...]
        u_ref[pl.ds(t0, tt), :] = _silu(y).astype(BF16)
        return 0

    lax.fori_loop(0, t // tt, conv, 0)


def _conv(proj, w_dw, b_dw, ln_g, ln_b, u_prev, *, n_seq, t, row0, n_tokens, cw):
    b0 = row0 // t
    assert row0 % t == 0
    ncol = proj.shape[1] // cw
    tt = 32
    in_specs = [
        pl.BlockSpec((t, cw), lambda b: (b0 + b, ncol - 2)),
        pl.BlockSpec((t, cw), lambda b: (b0 + b, ncol - 1)),
        pl.BlockSpec((CONV_W + 1, cw), lambda b: (0, 0)),
        pl.BlockSpec((1, cw), lambda b: (0, 0)),
        pl.BlockSpec((1, cw), lambda b: (0, 0)),
        pl.BlockSpec((1, cw), lambda b: (0, 0)),
    ]
    args = [proj, proj, w_dw, b_dw, ln_g, ln_b]
    aliases = {}
    if u_prev is not None:
        in_specs.append(pl.BlockSpec(memory_space=pl.ANY))
        args.append(u_prev)
        aliases = {len(args) - 1: 0}

    def kern(*refs):
        if u_prev is not None:
            refs = refs[:6] + refs[7:]
        _conv_kernel(*refs, t=t, tt=tt)

    return pl.pallas_call(
        kern,
        grid=(n_seq,),
        in_specs=in_specs,
        out_specs=pl.BlockSpec((t, cw), lambda b: (b0 + b, 0)),
        out_shape=jax.ShapeDtypeStruct((n_tokens, cw), BF16),
        scratch_shapes=[pltpu.VMEM((t + 2 * CONV_HALO, cw), F32)],
        input_output_aliases=aliases,
        compiler_params=_cparams(("arbitrary",)),
        name="conv",
    )(*args)


def _outproj_kernel(hm_ref, u_ref, x_ref, mod_ref, w_ref, o_ref, *, d):
    mw = hm_ref.shape[1]
    acc = jnp.dot(hm_ref[...], w_ref[0:mw, :], preferred_element_type=F32)
    acc = acc + jnp.dot(u_ref[...], w_ref[mw:, :], preferred_element_type=F32)
    o_ref[...] = x_ref[...] + mod_ref[:, 2 * d:3 * d] * acc


def _outproj(hm, u, x, mod, w_out, tm, row_of_tile):
    N, D = x.shape
    mw, cw = hm.shape[1], u.shape[1]
    return pl.pallas_call(
        functools.partial(_outproj_kernel, d=D),
        grid=(N // tm,),
        in_specs=[
            pl.BlockSpec((tm, mw), lambda i: (i, 0)),
            pl.BlockSpec((tm, cw), lambda i: (i, 0)),
            pl.BlockSpec((tm, D), lambda i: (i, 0)),
            pl.BlockSpec((None, 1, 6 * D), lambda i: (row_of_tile(i, tm), 0, 0)),
            pl.BlockSpec((mw + cw, D), lambda i: (0, 0)),
        ],
        out_specs=pl.BlockSpec((tm, D), lambda i: (i, 0)),
        out_shape=jax.ShapeDtypeStruct((N, D), F32),
        compiler_params=_cparams(("arbitrary",)),
        name="outproj",
    )(hm, u, x, mod, w_out)


def _ffn_kernel(x_ref, mod_ref, g_ref, wg_ref, wu_ref, wd_ref, o_ref, h_scr, acc_scr, *, d):
    j = pl.program_id(1)

    @pl.when(j == 0)
    def _():
        h_scr[...] = _rms_mod(x_ref[...], g_ref[...], mod_ref[:, 4 * d:5 * d], mod_ref[:, 3 * d:4 * d]).astype(BF16)
        acc_scr[...] = jnp.zeros_like(acc_scr)

    h = h_scr[...]
    a = jnp.dot(h, wg_ref[...], preferred_element_type=F32)
    b = jnp.dot(h, wu_ref[...], preferred_element_type=F32)
    acc_scr[...] += jnp.dot((_silu(a) * b).astype(BF16), wd_ref[...], preferred_element_type=F32)

    @pl.when(j == pl.num_programs(1) - 1)
    def _():
        o_ref[...] = x_ref[...] + mod_ref[:, 5 * d:6 * d] * acc_scr[...]


def _ffn(x, mod, g, wg, wu, wd, tm, tf, row_of_tile):
    N, D = x.shape
    ff = wg.shape[1]
    return pl.pallas_call(
        functools.partial(_ffn_kernel, d=D),
        grid=(N // tm, ff // tf),
        in_specs=[
            pl.BlockSpec((tm, D), lambda i, j: (i, 0)),
            pl.BlockSpec((None, 1, 6 * D), lambda i, j: (row_of_tile(i, tm), 0, 0)),
            pl.BlockSpec((1, D), lambda i, j: (0, 0)),
            pl.BlockSpec((D, tf), lambda i, j: (0, j)),
            pl.BlockSpec((D, tf), lambda i, j: (0, j)),
            pl.BlockSpec((tf, D), lambda i, j: (j, 0)),
        ],
        out_specs=pl.BlockSpec((tm, D), lambda i, j: (i, 0)),
        out_shape=jax.ShapeDtypeStruct((N, D), F32),
        scratch_shapes=[pltpu.VMEM((tm, D), BF16), pltpu.VMEM((tm, D), F32)],
        compiler_params=_cparams(("arbitrary", "arbitrary")),
        name="ffn",
    )(x, mod, g, wg, wu, wd)


def _router_kernel(x_ref, mod_ref, g_ref, wr_ref, h_ref, route_ref, cnt_ref, carry, *, d):
    i = pl.program_id(0)

    @pl.when(i == 0)
    def _():
        carry[...] = jnp.zeros_like(carry)

    h = _rms_mod(x_ref[...], g_ref[...], mod_ref[:, 4 * d:5 * d], mod_ref[:, 3 * d:4 * d])
    h_ref[...] = h
    tm = h.shape[0]
    logits = jnp.dot(h.astype(BF16), wr_ref[...], preferred_element_type=F32)
    lane = lax.broadcasted_iota(jnp.int32, (tm, LANES), 1)
    lg = jnp.where(lane < N_EXPERTS, logits, -jnp.inf)
    m1 = jnp.max(lg, axis=1, keepdims=True)
    i1 = jnp.min(jnp.where(lg == m1, lane, LANES), axis=1, keepdims=True)
    lg2 = jnp.where(lane == i1, -jnp.inf, lg)
    m2 = jnp.max(lg2, axis=1, keepdims=True)
    i2 = jnp.min(jnp.where(lg2 == m2, lane, LANES), axis=1, keepdims=True)
    e = jnp.exp(m2 - m1)
    w1 = 1.0 / (1.0 + e)
    w2 = e / (1.0 + e)
    sel1 = lane == i1
    sel2 = lane == i2
    onehot = (sel1 | sel2).astype(BF16)
    rr = lax.broadcasted_iota(jnp.int32, (tm, tm), 0)
    cc = lax.broadcasted_iota(jnp.int32, (tm, tm), 1)
    before = (cc < rr).astype(BF16)
    rank = jnp.dot(before, onehot, preferred_element_type=F32) + carry[...]
    r1 = jnp.sum(jnp.where(sel1, rank, 0.0), axis=1, keepdims=True)
    r2 = jnp.sum(jnp.where(sel2, rank, 0.0), axis=1, keepdims=True)
    carry[...] += jnp.sum(onehot.astype(F32), axis=0, keepdims=True)
    vals = (i1.astype(F32), i2.astype(F32), w1, w2, r1, r2)
    out = jnp.zeros((tm, LANES), F32)
    for k, v in enumerate(vals):
        out = jnp.where(lane == k, v, out)
    route_ref[...] = out
    cnt_ref[...] = jnp.broadcast_to(carry[...], cnt_ref.shape)


def _router(x, mod, g, w_router, tm, row_of_tile):
    N, D = x.shape
    return pl.pallas_call(
        functools.partial(_router_kernel, d=D),
        grid=(N // tm,),
        in_specs=[
            pl.BlockSpec((tm, D), lambda i: (i, 0)),
            pl.BlockSpec((None, 1, 6 * D), lambda i: (row_of_tile(i, tm), 0, 0)),
            pl.BlockSpec((1, D), lambda i: (0, 0)),
            pl.BlockSpec((D, LANES), lambda i: (0, 0)),
        ],
        out_specs=[
            pl.BlockSpec((tm, D), lambda i: (i, 0)),
            pl.BlockSpec((tm, LANES), lambda i: (i, 0)),
            pl.BlockSpec((8, LANES), lambda i: (0, 0)),
        ],
        out_shape=[
            jax.ShapeDtypeStruct((N, D), F32),
            jax.ShapeDtypeStruct((N, LANES), F32),
            jax.ShapeDtypeStruct((8, LANES), F32),
        ],
        scratch_shapes=[pltpu.VMEM((1, LANES), F32)],
        compiler_params=_cparams(("arbitrary",)),
        name="router",
    )(x, mod, g, w_router)


def _dispatch_kernel(p1_ref, p2_ref, h_ref, xs_in_ref, xs_ref, sem, *, tm):
    del xs_in_ref
    base = pl.program_id(0) * tm

    def issue(t, _):
        src = h_ref.at[pl.ds(base + t, 1)]
        pltpu.make_async_copy(src, xs_ref.at[pl.ds(p1_ref[t], 1)], sem).start()
        pltpu.make_async_copy(src, xs_ref.at[pl.ds(p2_ref[t], 1)], sem).start()
        return 0

    lax.fori_loop(0, tm, issue, 0)

    def drain(t, _):
        pltpu.make_async_copy(h_ref.at[pl.ds(0, 1)], xs_ref.at[pl.ds(0, 1)], sem).wait()
        return 0

    lax.fori_loop(0, 2 * tm, drain, 0)


def _dispatch(h, pos1, pos2, xs_init, tm):
    N, D = h.shape
    return pl.pallas_call(
        functools.partial(_dispatch_kernel, tm=tm),
        grid=(N // tm,),
        in_specs=[
            pl.BlockSpec((tm,), lambda i: (i,), memory_space=pltpu.SMEM),
            pl.BlockSpec((tm,), lambda i: (i,), memory_space=pltpu.SMEM),
            pl.BlockSpec(memory_space=pl.ANY),
            pl.BlockSpec(memory_space=pl.ANY),
        ],
        out_specs=pl.BlockSpec(memory_space=pl.ANY),
        out_shape=jax.ShapeDtypeStruct(xs_init.shape, xs_init.dtype),
        scratch_shapes=[pltpu.SemaphoreType.DMA(())],
        input_output_aliases={3: 0},
        compiler_params=_cparams(("arbitrary",)),
        name="dispatch",
    )(pos1, pos2, h, xs_init)


def _moe_kernel(te_ref, nv_ref, x_ref, wg_ref, wu_ref, wd_ref, o_ref, xb_scr, acc_scr):
    r = pl.program_id(0)
    j = pl.program_id(1)
    valid = r < nv_ref[0]
    last = j == pl.num_programs(1) - 1

    @pl.when(valid & (j == 0))
    def _():
        xb_scr[...] = x_ref[...].astype(BF16)
        acc_scr[...] = jnp.zeros_like(acc_scr)

    @pl.when(valid)
    def _():
        h = xb_scr[...]
        a = jnp.dot(h, wg_ref[...], preferred_element_type=F32)
        b = jnp.dot(h, wu_ref[...], preferred_element_type=F32)
        acc_scr[...] += jnp.dot((_silu(a) * b).astype(BF16), wd_ref[...], preferred_element_type=F32)

    @pl.when(valid & last)
    def _():
        o_ref[...] = acc_scr[...]

    @pl.when(jnp.logical_not(valid) & last)
    def _():
        o_ref[...] = jnp.zeros_like(o_ref)


def _moe(xs, tile_expert, n_valid, wg, wu, wd, tm, tf):
    P, D = xs.shape
    ff = wg.shape[2]
    nj = ff // tf

    def jj(r, j, nv):
        return jnp.where(r < nv[0], j, nj - 1)

    grid_spec = pltpu.PrefetchScalarGridSpec(
        num_scalar_prefetch=2,
        grid=(P // tm, nj),
        in_specs=[
            pl.BlockSpec((tm, D), lambda r, j, te, nv: (r, 0)),
            pl.BlockSpec((None, D, tf), lambda r, j, te, nv: (te[r], 0, jj(r, j, nv))),
            pl.BlockSpec((None, D, tf), lambda r, j, te, nv: (te[r], 0, jj(r, j, nv))),
            pl.BlockSpec((None, tf, D), lambda r, j, te, nv: (te[r], jj(r, j, nv), 0)),
        ],
        out_specs=pl.BlockSpec((tm, D), lambda r, j, te, nv: (r, 0)),
        scratch_shapes=[pltpu.VMEM((tm, D), BF16), pltpu.VMEM((tm, D), F32)],
    )
    return pl.pallas_call(
        _moe_kernel,
        grid_spec=grid_spec,
        out_shape=jax.ShapeDtypeStruct((P, D), F32),
        compiler_params=_cparams(("arbitrary", "arbitrary")),
        name="moe",
    )(tile_expert, n_valid, xs, wg, wu, wd)


def _combine_kernel(p1_ref, p2_ref, x_ref, route_ref, mod_ref, gfin_ref, ys_ref, o_ref, y_scr, sem,
                    *, d, tm, final_norm):
    def issue(t, _):
        pltpu.make_async_copy(ys_ref.at[pl.ds(p1_ref[t], 1)], y_scr.at[0, pl.ds(t, 1)], sem).start()
        pltpu.make_async_copy(ys_ref.at[pl.ds(p2_ref[t], 1)], y_scr.at[1, pl.ds(t, 1)], sem).start()
        return 0

    lax.fori_loop(0, tm, issue, 0)

    def drain(t, _):
        pltpu.make_async_copy(ys_ref.at[pl.ds(0, 1)], y_scr.at[0, pl.ds(0, 1)], sem).wait()
        return 0

    lax.fori_loop(0, 2 * tm, drain, 0)
    route = route_ref[...]
    f = route[:, 2:3] * y_scr[0] + route[:, 3:4] * y_scr[1]
    y = x_ref[...] + mod_ref[:, 5 * d:6 * d] * f
    if final_norm:
        y = y * lax.rsqrt(jnp.mean(y * y, axis=-1, keepdims=True) + EPS) * gfin_ref[...]
    o_ref[...] = y


def _combine(x, route, mod, g_final, ys, pos1, pos2, tm, row_of_tile, final_norm):
    N, D = x.shape
    return pl.pallas_call(
        functools.partial(_combine_kernel, d=D, tm=tm, final_norm=final_norm),
        grid=(N // tm,),
        in_specs=[
            pl.BlockSpec((tm,), lambda i: (i,), memory_space=pltpu.SMEM),
            pl.BlockSpec((tm,), lambda i: (i,), memory_space=pltpu.SMEM),
            pl.BlockSpec((tm, D), lambda i: (i, 0)),
            pl.BlockSpec((tm, LANES), lambda i: (i, 0)),
            pl.BlockSpec((None, 1, 6 * D), lambda i: (row_of_tile(i, tm), 0, 0)),
            pl.BlockSpec((1, D), lambda i: (0, 0)),
            pl.BlockSpec(memory_space=pl.ANY),
        ],
        out_specs=pl.BlockSpec((tm, D), lambda i: (i, 0)),
        out_shape=jax.ShapeDtypeStruct((N, D), F32),
        scratch_shapes=[pltpu.VMEM((2, tm, D), F32), pltpu.SemaphoreType.DMA(())],
        compiler_params=_cparams(("arbitrary",)),
        name="combine",
    )(pos1, pos2, x, route, mod, g_final, ys)


def _final_kernel(x_ref, g_ref, o_ref):
    x = x_ref[...]
    o_ref[...] = x * lax.rsqrt(jnp.mean(x * x, axis=-1, keepdims=True) + EPS) * g_ref[...]


def _final_norm(x, g, tm):
    N, D = x.shape
    return pl.pallas_call(
        _final_kernel,
        grid=(N // tm,),
        in_specs=[pl.BlockSpec((tm, D), lambda i: (i, 0)), pl.BlockSpec((1, D), lambda i: (0, 0))],
        out_specs=pl.BlockSpec((tm, D), lambda i: (i, 0)),
        out_shape=jax.ShapeDtypeStruct((N, D), F32),
        compiler_params=_cparams(("arbitrary",)),
        name="final_norm",
    )(x, g)


def _grid_sincos(t, grid_w, d):
    rows = t // grid_w
    row = jnp.repeat(jnp.arange(rows, dtype=F32), grid_w)
    col = jnp.tile(jnp.arange(grid_w, dtype=F32), rows)
    quarter = d // 4
    omega = 1.0 / (10000.0 ** (jnp.arange(quarter, dtype=F32) / quarter))

    def emb(p):
        a = p[:, None] * omega[None, :]
        return jnp.concatenate([jnp.sin(a), jnp.cos(a)], axis=-1)

    return jnp.concatenate([emb(row), emb(col)], axis=-1)


def _tile_plan(t_prompt_total, t_seq):
    tm = 512
    while t_prompt_total % tm or t_seq % tm:
        tm //= 2
    return tm


def kernel(x_prompt, x_sample, state_C, state_n, state_m, c, c_ctx, w_ada, b_ada, g_mix, w_in, b_gate, g_mh,
           w_dw, b_dw, ln_g, ln_b, w_out, g_ffn, w_ff_gate, w_ff_up, w_ff_down, w_router, w_moe_gate, w_moe_up,
           w_moe_down, g_final):
    B, S, D = x_prompt.shape
    DB, DS, _ = x_sample.shape
    depth = w_ada.shape[0]
    H, Dh = M_HEADS, HEAD_DIM
    mw = H * Dh
    cw = w_dw.shape[2]
    n_gates = 4 * H
    NP, NS = B * S, DB * DS
    N = NP + NS
    grid_w = 64
    assert NP % DS == 0 or DS % NP == 0
    tm = _tile_plan(NP, DS)

    def row_of_tile(i, tile):
        start = i * tile
        return jnp.where(start < NP, 0, 1 + (start - NP) // DS)

    xs = x_sample + _grid_sincos(DS, grid_w, D)[None]
    x = jnp.concatenate([x_prompt.reshape(NP, D), xs.reshape(NS, D)], axis=0)

    n_cond = 1 + DB
    r_pad = -(-n_cond // 8) * 8
    cond = jnp.zeros((r_pad, D), F32).at[0].set(c_ctx).at[1:n_cond].set(c)
    mods = _ada(cond, w_ada, b_ada).reshape(depth, r_pad, 1, 6 * D)

    zero_c = jnp.zeros((B, 2, H, Dh, Dh), F32)
    zero_n = jnp.zeros((B, 2, H, 1, LANES), F32)
    neg_m = jnp.full((B, 2, H, 1, LANES), NEG, F32)

    new_c, new_n, new_m = [], [], []
    for l in range(depth):
        mod = mods[l]
        w_l = w_in[l]
        w_main = w_l[:, :4 * mw + 2 * cw].astype(BF16)
        w_gate = jnp.pad(w_l[:, 4 * mw + 2 * cw:], ((0, 0), (0, LANES - n_gates))).astype(BF16)
        bg = jnp.pad(b_gate[l], (0, LANES - n_gates)).reshape(1, LANES)
        proj, gates = _inproj(x, mod, g_mix[l].reshape(1, D), w_main, w_gate, bg, tm, row_of_tile)

        gr = gates[:, :n_gates].reshape(N, 4, H).transpose(2, 1, 0)
        gr = jnp.pad(gr, ((0, 0), (0, 4), (0, 0)))

        gmh = g_mh[l].reshape(1, mw)
        hm, c_p, n_p, m_p = _mlstm(proj, gr, gmh, zero_c, zero_n, neg_m, None,
                                   n_seq=B, t=S, row0=0, n_tokens=N)
        c_s = state_C[:, l]
        n_s = jnp.broadcast_to(state_n[:, l][:, :, :, None, :], (DB, 2, H, 1, Dh))
        m_s = jnp.broadcast_to(state_m[:, l][:, :, :, None, None], (DB, 2, H, 1, LANES))
        hm, _, _, _ = _mlstm(proj, gr, gmh, c_s, n_s, m_s, hm, n_seq=DB, t=DS, row0=NP, n_tokens=N)
        new_c.append(c_p)
        new_n.append(n_p[:, :, :, 0, :])
        new_m.append(m_p[:, :, :, 0, 0])

        wdw = jnp.pad(w_dw[l], ((0, 1), (0, 0)))
        conv_args = (wdw, b_dw[l].reshape(1, cw), ln_g[l].reshape(1, cw), ln_b[l].reshape(1, cw))
        u = _conv(proj, *conv_args, None, n_seq=B, t=S, row0=0, n_tokens=N, cw=cw)
        u = _conv(proj, *conv_args, u, n_seq=DB, t=DS, row0=NP, n_tokens=N, cw=cw)

        x = _outproj(hm, u, x, mod, w_out[l].astype(BF16), tm, row_of_tile)

        gf = g_ffn[l].reshape(1, D)
        if l % 2 == 0:
            e = l // 2
            ff = w_ff_gate.shape[2]
            tf = ff // 2 if (ff // 2) % LANES == 0 else ff
            x = _ffn(x, mod, gf, w_ff_gate[e].astype(BF16), w_ff_up[e].astype(BF16),
                     w_ff_down[e].astype(BF16), tm, tf, row_of_tile)
        else:
            e = l // 2
            wr = jnp.pad(w_router[e], ((0, 0), (0, LANES - N_EXPERTS))).astype(BF16)
            h, route, cnt = _router(x, mod, gf, wr, tm, row_of_tile)
            tme = tm
            counts = cnt[0, :N_EXPERTS].astype(jnp.int32)
            padded = ((counts + tme - 1) // tme) * tme
            ends = jnp.cumsum(padded)
            offs = ends - padded
            e1 = route[:, 0].astype(jnp.int32)
            e2 = route[:, 1].astype(jnp.int32)
            pos1 = offs[e1] + route[:, 4].astype(jnp.int32)
            pos2 = offs[e2] + route[:, 5].astype(jnp.int32)
            n_tiles = (2 * N) // tme + N_EXPERTS
            starts = jnp.arange(n_tiles, dtype=jnp.int32) * tme
            tile_expert = jnp.minimum(jnp.sum(starts[:, None] >= ends[None, :], axis=1), N_EXPERTS - 1)
            n_valid = (ends[-1] // tme).reshape(1)
            xs_sorted = _dispatch(h, pos1, pos2, jnp.zeros((n_tiles * tme, D), F32), tm)
            ffe = w_moe_gate.shape[3]
            tfe = 512 if ffe % 512 == 0 else ffe
            ys = _moe(xs_sorted, tile_expert.astype(jnp.int32), n_valid.astype(jnp.int32),
                      w_moe_gate[e].astype(BF16), w_moe_up[e].astype(BF16), w_moe_down[e].astype(BF16), tme, tfe)
            x = _combine(x, route, mod, g_final.reshape(1, D), ys, pos1, pos2, min(tm, 256), row_of_tile,
                         final_norm=(l == depth - 1))

    if (depth - 1) % 2 == 0:
        x = _final_norm(x, g_final.reshape(1, D), tm)

    y_prompt = x[:NP].reshape(B, S, D)
    y_sample = x[NP:].reshape(DB, DS, D)
    new_state_c = jnp.stack(new_c, axis=1)
    new_state_n = jnp.stack(new_n, axis=1)
    new_state_m = jnp.stack(new_m, axis=1)
    return (y_prompt, y_sample, new_state_c, new_state_n, new_state_m)
```

```python
import functools

import jax
import jax.numpy as jnp
from jax import lax
from jax.experimental import pallas as pl
from jax.experimental.pallas import tpu as pltpu

F32 = jnp.float32
BF16 = jnp.bfloat16
EPS = 1e-6
NEG = -1e30

M_HEADS = 4
HEAD_DIM = 128
CHUNK = 128
CONV_W = 31
CONV_HALO = 16
MLSTM_UNROLL = 4
CONV_ROWS = 32
COMBINE_ROWS = 256
N_EXPERTS = 8
LANES = 128
SUBLANES = 8
VMEM_LIMIT = 56 * 1024 * 1024


def _cparams(sem):
    return pltpu.CompilerParams(dimension_semantics=sem, vmem_limit_bytes=VMEM_LIMIT)


def _silu(a):
    return a * jax.nn.sigmoid(a)


def _rms_mod(x, g, scale, shift):
    y = x * lax.rsqrt(jnp.mean(x * x, axis=-1, keepdims=True) + EPS) * g
    return y * (1.0 + scale) + shift


def _ada_kernel(c_ref, w_ref, b_ref, o_ref):
    s = _silu(c_ref[...]).astype(BF16)
    o_ref[...] = jnp.dot(s, w_ref[...].astype(BF16), preferred_element_type=F32) + b_ref[...]


def _ada(cond, w_ada, b_ada):
    R, D = cond.shape
    L, _, D6 = w_ada.shape
    tn = 1536
    return pl.pallas_call(
        _ada_kernel,
        grid=(L, D6 // tn),
        in_specs=[
            pl.BlockSpec((R, D), lambda l, j: (0, 0)),
            pl.BlockSpec((None, D, tn), lambda l, j: (l, 0, j)),
            pl.BlockSpec((None, 1, tn), lambda l, j: (l, 0, j)),
        ],
        out_specs=pl.BlockSpec((None, R, tn), lambda l, j: (l, 0, j)),
        out_shape=jax.ShapeDtypeStruct((L, R, D6), F32),
        compiler_params=_cparams(("arbitrary", "arbitrary")),
        name="ada",
    )(cond, w_ada, b_ada.reshape(L, 1, D6))


def _inproj_kernel(x_ref, mod_ref, g_ref, w_ref, wg_ref, bg_ref, proj_ref, gates_ref, *, d, k_scale):
    h = _rms_mod(x_ref[...], g_ref[...], mod_ref[:, d:2 * d], mod_ref[:, 0:d]).astype(BF16)
    nw = w_ref.shape[1]
    cw = 512
    for j in range(nw // cw):
        r = jnp.dot(h, w_ref[:, j * cw:(j + 1) * cw], preferred_element_type=F32)
        if j == 1:
            r = r * k_scale
        proj_ref[:, j * cw:(j + 1) * cw] = r.astype(BF16)
    gates_ref[...] = jnp.dot(h, wg_ref[...], preferred_element_type=F32) + bg_ref[...]


def _inproj(x, mod, g, w_main, w_gate, b_gate, tm, row_of_tile):
    N, D = x.shape
    nw = w_main.shape[1]
    return pl.pallas_call(
        functools.partial(_inproj_kernel, d=D, k_scale=HEAD_DIM ** -0.5),
        grid=(N // tm,),
        in_specs=[
            pl.BlockSpec((tm, D), lambda i: (i, 0)),
            pl.BlockSpec((None, 1, 6 * D), lambda i: (row_of_tile(i, tm), 0, 0)),
            pl.BlockSpec((1, D), lambda i: (0, 0)),
            pl.BlockSpec((D, nw), lambda i: (0, 0)),
            pl.BlockSpec((D, LANES), lambda i: (0, 0)),
            pl.BlockSpec((1, LANES), lambda i: (0, 0)),
        ],
        out_specs=[
            pl.BlockSpec((tm, nw), lambda i: (i, 0)),
            pl.BlockSpec((tm, LANES), lambda i: (i, 0)),
        ],
        out_shape=[
            jax.ShapeDtypeStruct((N, nw), BF16),
            jax.ShapeDtypeStruct((N, LANES), F32),
        ],
        compiler_params=_cparams(("arbitrary",)),
        name="inproj",
    )(x, mod, g, w_main, w_gate, b_gate)


def _mlstm_step(qc, kc, vc, i_row, b_row, i_col, b_col, btot, mask, c_ref, n, m):
    c_mat = c_ref[...]
    dmat = jnp.where(mask, b_col - b_row + i_row, NEG)
    inter = b_col + m
    m_t = jnp.maximum(inter, jnp.max(dmat, axis=1, keepdims=True))
    s = lax.dot_general(qc, kc, (((1,), (1,)), ((), ())), preferred_element_type=F32) * jnp.exp(dmat - m_t)
    w_inter = jnp.exp(inter - m_t)
    num = (jnp.dot(s.astype(BF16), vc, preferred_element_type=F32)
           + w_inter * jnp.dot(qc, c_mat.astype(BF16), preferred_element_type=F32))
    qn = (jnp.sum(s, axis=1, keepdims=True)
          + w_inter * jnp.sum(qc.astype(F32) * n, axis=1, keepdims=True))
    h = num / jnp.maximum(jnp.abs(qn), jnp.exp(-m_t))
    w_log = btot - b_row + i_row
    m_new = jnp.maximum(btot + m, jnp.max(w_log, axis=1, keepdims=True))
    decay = jnp.exp(btot + m - m_new)
    kw = kc.astype(F32) * jnp.exp(btot - b_col + i_col - m_new)
    c_ref[...] = decay * c_mat + lax.dot_general(kw.astype(BF16), vc, (((0,), (0,)), ((), ())),
                                                 preferred_element_type=F32)
    n_new = decay * n + jnp.sum(kw, axis=0, keepdims=True)
    return h, n_new, m_new


def _mlstm_kernel(q_ref, k_ref, v_ref, o_ref, gr_ref, gmh_ref, c0_ref, n0_ref, m0_ref,
                  hm_ref, ct_ref, nt_ref, mt_ref, hf_ref, hb_ref, *, nc, unroll):
    L = CHUNK
    row = lax.broadcasted_iota(jnp.int32, (L, L), 0)
    col = lax.broadcasted_iota(jnp.int32, (L, L), 1)
    lower = col <= row
    upper = col >= row
    cum_f = upper.astype(F32)
    cum_b = lower.astype(F32)
    rid = lax.broadcasted_iota(jnp.int32, (8, L), 0)
    zpad = jnp.zeros((L - 8, L), F32)

    ct_ref[...] = c0_ref[...]

    def gate_tiles(c):
        g = gr_ref[:, pl.ds(pl.multiple_of(c * L, L), L)]
        lf = jax.nn.log_sigmoid(g)
        bf = jnp.dot(lf, cum_f, preferred_element_type=F32, precision=lax.Precision.HIGHEST)
        bb = jnp.dot(lf, cum_b, preferred_element_type=F32, precision=lax.Precision.HIGHEST)
        r = jnp.where(rid == 1, bf, jnp.where(rid == 3, bb, g))
        tot = jnp.sum(jnp.where((rid == 1) | (rid == 3), lf, 0.0), axis=1, keepdims=True)
        rt = jnp.concatenate([r, zpad], axis=0).T
        return r, rt, tot

    def body(c, carry):
        nf, mf, nb, mb = carry
        cf = c
        cb = nc - 1 - c
        r, rt, tot = gate_tiles(cf)
        sl = pl.ds(pl.multiple_of(cf * L, L), L)
        h, nf, mf = _mlstm_step(q_ref[sl, :], k_ref[sl, :], v_ref[sl, :],
                                r[0:1, :], r[1:2, :], rt[:, 0:1], rt[:, 1:2], tot[1:2, :],
                                lower, ct_ref.at[0], nf, mf)
        hf_ref[sl, :] = h
        r, rt, tot = gate_tiles(cb)
        sl = pl.ds(pl.multiple_of(cb * L, L), L)
        h, nb, mb = _mlstm_step(q_ref[sl, :], k_ref[sl, :], v_ref[sl, :],
                                r[2:3, :], r[3:4, :], rt[:, 2:3], rt[:, 3:4], tot[3:4, :],
                                upper, ct_ref.at[1], nb, mb)
        hb_ref[sl, :] = h
        return nf, mf, nb, mb

    init = (n0_ref[0], m0_ref[0][:, 0:1], n0_ref[1], m0_ref[1][:, 0:1])
    nf, mf, nb, mb = lax.fori_loop(0, nc, body, init, unroll=unroll)
    nt_ref[0] = nf
    nt_ref[1] = nb
    mt_ref[0] = jnp.broadcast_to(mf, (1, LANES))
    mt_ref[1] = jnp.broadcast_to(mb, (1, LANES))

    gmh = gmh_ref[...]

    def fin(c, _):
        sl = pl.ds(pl.multiple_of(c * L, L), L)
        hm = hf_ref[sl, :] + hb_ref[sl, :]
        mu = jnp.mean(hm, axis=-1, keepdims=True)
        var = jnp.mean(jnp.square(hm - mu), axis=-1, keepdims=True)
        y = (hm - mu) * lax.rsqrt(var + EPS) * gmh
        hm_ref[sl, :] = (jax.nn.sigmoid(o_ref[sl, :].astype(F32)) * y).astype(BF16)
        return 0

    lax.fori_loop(0, nc, fin, 0)


def _mlstm(proj, gr, g_mh, c0, n0, m0, *, n_seq, t, row0):
    H, Dh = M_HEADS, HEAD_DIM
    b0 = row0 // t
    assert row0 % t == 0 and t % CHUNK == 0
    nc = t // CHUNK

    def pspec(part):
        return pl.BlockSpec((t, Dh), lambda b, h: (b0 + b, part * H + h))

    st_spec_c = pl.BlockSpec((None, 2, None, Dh, Dh), lambda b, h: (b, 0, h, 0, 0))
    st_spec_v = pl.BlockSpec((None, 2, None, 1, LANES), lambda b, h: (b, 0, h, 0, 0))
    return pl.pallas_call(
        functools.partial(_mlstm_kernel, nc=nc, unroll=min(nc, MLSTM_UNROLL)),
        grid=(n_seq, H),
        in_specs=[pspec(0), pspec(1), pspec(2), pspec(3),
                  pl.BlockSpec((None, 8, t), lambda b, h: (h, 0, b0 + b)),
                  pl.BlockSpec((1, Dh), lambda b, h: (0, h)),
                  st_spec_c, st_spec_v, st_spec_v],
        out_specs=[
            pl.BlockSpec((t, Dh), lambda b, h: (b, h)),
            st_spec_c, st_spec_v, st_spec_v,
        ],
        out_shape=[
            jax.ShapeDtypeStruct((n_seq * t, H * Dh), BF16),
            jax.ShapeDtypeStruct((n_seq, 2, H, Dh, Dh), F32),
            jax.ShapeDtypeStruct((n_seq, 2, H, 1, LANES), F32),
            jax.ShapeDtypeStruct((n_seq, 2, H, 1, LANES), F32),
        ],
        scratch_shapes=[pltpu.VMEM((t, Dh), F32), pltpu.VMEM((t, Dh), F32)],
        compiler_params=_cparams(("arbitrary", "arbitrary")),
        name="mlstm",
    )(proj, proj, proj, proj, gr, g_mh, c0, n0, m0)


def _conv_kernel(ga_ref, gb_ref, w_ref, bdw_ref, lng_ref, lnb_ref, u_ref, pad_ref, *, t, tt):
    cw = pad_ref.shape[1]
    zeros = jnp.zeros((CONV_HALO, cw), F32)
    pad_ref[0:CONV_HALO, :] = zeros
    pad_ref[CONV_HALO + t:2 * CONV_HALO + t, :] = zeros

    def fill(i, _):
        sl = pl.ds(pl.multiple_of(i * tt, tt), tt)
        glu = ga_ref[sl, :].astype(F32) * jax.nn.sigmoid(gb_ref[sl, :].astype(F32))
        pad_ref[pl.ds(pl.multiple_of(CONV_HALO + i * tt, 8), tt), :] = glu
        return 0

    lax.fori_loop(0, t // tt, fill, 0)
    shift0 = CONV_HALO - CONV_W // 2
    rows = tt + 2 * CONV_HALO

    def conv(i, _):
        t0 = pl.multiple_of(i * tt, tt)
        accs = []
        for lg in range(cw // LANES):
            ls = slice(lg * LANES, (lg + 1) * LANES)
            win = pad_ref[pl.ds(t0, rows), ls]
            acc = jnp.broadcast_to(bdw_ref[:, ls], (tt, LANES))
            for r in range(8):
                wr = win if r == 0 else pltpu.roll(win, rows - r, axis=0)
                for a in range(4):
                    j = 8 * a + r - shift0
                    if 0 <= j < CONV_W:
                        acc = acc + wr[8 * a:8 * a + tt, :] * w_ref[j:j + 1, ls]
            accs.append(acc)
        acc = jnp.concatenate(accs, axis=1)
        mu = jnp.mean(acc, axis=-1, keepdims=True)
        var = jnp.mean(jnp.square(acc - mu), axis=-1, keepdims=True)
        y = (acc - mu) * lax.rsqrt(var + EPS) * lng_ref[...] + lnb_ref[...]
        u_ref[pl.ds(t0, tt), :] = _silu(y).astype(BF16)
        return 0

    lax.fori_loop(0, t // tt, conv, 0)


def _conv(proj, w_dw, b_dw, ln_g, ln_b, *, n_seq, t, row0, cw):
    b0 = row0 // t
    assert row0 % t == 0
    ncol = proj.shape[1] // cw
    return pl.pallas_call(
        functools.partial(_conv_kernel, t=t, tt=CONV_ROWS),
        grid=(n_seq,),
        in_specs=[
            pl.BlockSpec((t, cw), lambda b: (b0 + b, ncol - 2)),
            pl.BlockSpec((t, cw), lambda b: (b0 + b, ncol - 1)),
            pl.BlockSpec((CONV_W + 1, cw), lambda b: (0, 0)),
            pl.BlockSpec((1, cw), lambda b: (0, 0)),
            pl.BlockSpec((1, cw), lambda b: (0, 0)),
            pl.BlockSpec((1, cw), lambda b: (0, 0)),
        ],
        out_specs=pl.BlockSpec((t, cw), lambda b: (b, 0)),
        out_shape=jax.ShapeDtypeStruct((n_seq * t, cw), BF16),
        scratch_shapes=[pltpu.VMEM((t + 2 * CONV_HALO, cw), F32)],
        compiler_params=_cparams(("arbitrary",)),
        name="conv",
    )(proj, proj, w_dw, b_dw, ln_g, ln_b)


def _outproj_kernel(hmp_ref, up_ref, hms_ref, us_ref, x_ref, mod_ref, w_ref, o_ref, *, d, n_prompt_tiles):
    mw = hmp_ref.shape[1]

    def project(hm_ref, u_ref):
        acc = jnp.dot(hm_ref[...], w_ref[0:mw, :], preferred_element_type=F32)
        acc = acc + jnp.dot(u_ref[...], w_ref[mw:, :], preferred_element_type=F32)
        o_ref[...] = x_ref[...] + mod_ref[:, 2 * d:3 * d] * acc

    is_prompt = pl.program_id(0) < n_prompt_tiles
    pl.when(is_prompt)(lambda: project(hmp_ref, up_ref))
    pl.when(jnp.logical_not(is_prompt))(lambda: project(hms_ref, us_ref))


def _outproj(hm_p, u_p, hm_s, u_s, x, mod, w_out, tm, row_of_tile):
    N, D = x.shape
    mw, cw = hm_p.shape[1], u_p.shape[1]
    npt = hm_p.shape[0] // tm
    return pl.pallas_call(
        functools.partial(_outproj_kernel, d=D, n_prompt_tiles=npt),
        grid=(N // tm,),
        in_specs=[
            pl.BlockSpec((tm, mw), lambda i: (jnp.minimum(i, npt - 1), 0)),
            pl.BlockSpec((tm, cw), lambda i: (jnp.minimum(i, npt - 1), 0)),
            pl.BlockSpec((tm, mw), lambda i: (jnp.maximum(i - npt, 0), 0)),
            pl.BlockSpec((tm, cw), lambda i: (jnp.maximum(i - npt, 0), 0)),
            pl.BlockSpec((tm, D), lambda i: (i, 0)),
            pl.BlockSpec((None, 1, 6 * D), lambda i: (row_of_tile(i, tm), 0, 0)),
            pl.BlockSpec((mw + cw, D), lambda i: (0, 0)),
        ],
        out_specs=pl.BlockSpec((tm, D), lambda i: (i, 0)),
        out_shape=jax.ShapeDtypeStruct((N, D), F32),
        compiler_params=_cparams(("arbitrary",)),
        name="outproj",
    )(hm_p, u_p, hm_s, u_s, x, mod, w_out)


def _ffn_kernel(x_ref, mod_ref, g_ref, wg_ref, wu_ref, wd_ref, o_ref, h_scr, acc_scr, *, d):
    j = pl.program_id(1)

    @pl.when(j == 0)
    def _():
        h_scr[...] = _rms_mod(x_ref[...], g_ref[...], mod_ref[:, 4 * d:5 * d], mod_ref[:, 3 * d:4 * d]).astype(BF16)
        acc_scr[...] = jnp.zeros_like(acc_scr)

    h = h_scr[...]
    a = jnp.dot(h, wg_ref[...], preferred_element_type=F32)
    b = jnp.dot(h, wu_ref[...], preferred_element_type=F32)
    acc_scr[...] += jnp.dot((_silu(a) * b).astype(BF16), wd_ref[...], preferred_element_type=F32)

    @pl.when(j == pl.num_programs(1) - 1)
    def _():
        o_ref[...] = x_ref[...] + mod_ref[:, 5 * d:6 * d] * acc_scr[...]


def _ffn(x, mod, g, wg, wu, wd, tm, tf, row_of_tile):
    N, D = x.shape
    ff = wg.shape[1]
    return pl.pallas_call(
        functools.partial(_ffn_kernel, d=D),
        grid=(N // tm, ff // tf),
        in_specs=[
            pl.BlockSpec((tm, D), lambda i, j: (i, 0)),
            pl.BlockSpec((None, 1, 6 * D), lambda i, j: (row_of_tile(i, tm), 0, 0)),
            pl.BlockSpec((1, D), lambda i, j: (0, 0)),
            pl.BlockSpec((D, tf), lambda i, j: (0, j)),
            pl.BlockSpec((D, tf), lambda i, j: (0, j)),
            pl.BlockSpec((tf, D), lambda i, j: (j, 0)),
        ],
        out_specs=pl.BlockSpec((tm, D), lambda i, j: (i, 0)),
        out_shape=jax.ShapeDtypeStruct((N, D), F32),
        scratch_shapes=[pltpu.VMEM((tm, D), BF16), pltpu.VMEM((tm, D), F32)],
        compiler_params=_cparams(("arbitrary", "arbitrary")),
        name="ffn",
    )(x, mod, g, wg, wu, wd)


def _router_kernel(x_ref, mod_ref, g_ref, wr_ref, h_ref, route_ref, cnt_ref, carry_ref, carry, *, d):
    i = pl.program_id(0)

    @pl.when(i == 0)
    def _():
        carry[...] = jnp.zeros_like(carry)

    h = _rms_mod(x_ref[...], g_ref[...], mod_ref[:, 4 * d:5 * d], mod_ref[:, 3 * d:4 * d])
    h_ref[...] = h
    tm = h.shape[0]
    logits = jnp.dot(h.astype(BF16), wr_ref[...], preferred_element_type=F32)
    lane = lax.broadcasted_iota(jnp.int32, (tm, LANES), 1)
    lg = jnp.where(lane < N_EXPERTS, logits, -jnp.inf)
    m1 = jnp.max(lg, axis=1, keepdims=True)
    i1 = jnp.min(jnp.where(lg == m1, lane, LANES), axis=1, keepdims=True)
    lg2 = jnp.where(lane == i1, -jnp.inf, lg)
    m2 = jnp.max(lg2, axis=1, keepdims=True)
    i2 = jnp.min(jnp.where(lg2 == m2, lane, LANES), axis=1, keepdims=True)
    e = jnp.exp(m2 - m1)
    w1 = 1.0 / (1.0 + e)
    w2 = e / (1.0 + e)
    sel1 = lane == i1
    sel2 = lane == i2
    onehot = (sel1 | sel2).astype(BF16)
    rr = lax.broadcasted_iota(jnp.int32, (tm, tm), 0)
    cc = lax.broadcasted_iota(jnp.int32, (tm, tm), 1)
    before = (cc < rr).astype(BF16)
    carry_ref[...] = jnp.broadcast_to(carry[...], carry_ref.shape)
    rank = jnp.dot(before, onehot, preferred_element_type=F32) + carry[...]
    r1 = jnp.sum(jnp.where(sel1, rank, 0.0), axis=1, keepdims=True)
    r2 = jnp.sum(jnp.where(sel2, rank, 0.0), axis=1, keepdims=True)
    carry[...] += jnp.sum(onehot.astype(F32), axis=0, keepdims=True)
    vals = (i1.astype(F32), i2.astype(F32), w1, w2, r1, r2)
    out = jnp.zeros((tm, LANES), F32)
    for k, v in enumerate(vals):
        out = jnp.where(lane == k, v, out)
    route_ref[...] = out
    cnt_ref[...] = jnp.broadcast_to(carry[...], cnt_ref.shape)


def _router(x, mod, g, w_router, tm, row_of_tile):
    N, D = x.shape
    return pl.pallas_call(
        functools.partial(_router_kernel, d=D),
        grid=(N // tm,),
        in_specs=[
            pl.BlockSpec((tm, D), lambda i: (i, 0)),
            pl.BlockSpec((None, 1, 6 * D), lambda i: (row_of_tile(i, tm), 0, 0)),
            pl.BlockSpec((1, D), lambda i: (0, 0)),
            pl.BlockSpec((D, LANES), lambda i: (0, 0)),
        ],
        out_specs=[
            pl.BlockSpec((tm, D), lambda i: (i, 0)),
            pl.BlockSpec((tm, LANES), lambda i: (i, 0)),
            pl.BlockSpec((8, LANES), lambda i: (0, 0)),
            pl.BlockSpec((None, 8, LANES), lambda i: (i, 0, 0)),
        ],
        out_shape=[
            jax.ShapeDtypeStruct((N, D), F32),
            jax.ShapeDtypeStruct((N, LANES), F32),
            jax.ShapeDtypeStruct((8, LANES), F32),
            jax.ShapeDtypeStruct((N // tm, 8, LANES), F32),
        ],
        scratch_shapes=[pltpu.VMEM((1, LANES), F32)],
        compiler_params=_cparams(("arbitrary",)),
        name="router",
    )(x, mod, g, w_router)


def _moe_kernel(te_ref, src0_ref, srcn_ref, h_ref, wg_ref, wu_ref, wd_ref, o_ref,
                xg, xb_scr, acc_scr, sem, *, tm, nj):
    del te_ref
    r = pl.program_id(0)
    j = pl.program_id(1)
    slot = r % 2
    other = 1 - slot
    ch = tm // nj

    def row_copy(idx_ref, row, s):
        return pltpu.make_async_copy(h_ref.at[pl.ds(idx_ref[row], 1)], xg.at[s, pl.ds(row, 1)], sem.at[s])

    def drain(s):
        def wait_one(t, c):
            row_copy(src0_ref, 0, s).wait()
            return c
        lax.fori_loop(0, tm, wait_one, 0)

    @pl.when((r == 0) & (j == 0))
    def _():
        def issue(t, c):
            row_copy(src0_ref, t, 0).start()
            return c
        lax.fori_loop(0, tm, issue, 0)

    @pl.when(j == 0)
    def _():
        drain(slot)
        xb_scr[...] = xg[slot].astype(BF16)
        acc_scr[...] = jnp.zeros_like(acc_scr)

    base = j * ch
    for t in range(ch):
        row_copy(srcn_ref, base + t, other).start()

    h = xb_scr[...]
    a = jnp.dot(h, wg_ref[...], preferred_element_type=F32)
    b = jnp.dot(h, wu_ref[...], preferred_element_type=F32)
    acc_scr[...] += jnp.dot((_silu(a) * b).astype(BF16), wd_ref[...], preferred_element_type=F32)

    @pl.when(j == nj - 1)
    def _():
        o_ref[...] = acc_scr[...]

    @pl.when((r == pl.num_programs(0) - 1) & (j == nj - 1))
    def _():
        drain(other)


def _moe(h, src, tile_expert, wg, wu, wd, tm, tf):
    P = src.shape[0]
    D = h.shape[1]
    ff = wg.shape[2]
    nj = ff // tf
    nr = P // tm
    assert tm % nj == 0
    grid_spec = pltpu.PrefetchScalarGridSpec(
        num_scalar_prefetch=1,
        grid=(nr, nj),
        in_specs=[
            pl.BlockSpec((tm,), lambda r, j, te: (0,), memory_space=pltpu.SMEM),
            pl.BlockSpec((tm,), lambda r, j, te: (jnp.minimum(r + 1, nr - 1),), memory_space=pltpu.SMEM),
            pl.BlockSpec(memory_space=pl.ANY),
            pl.BlockSpec((None, D, tf), lambda r, j, te: (te[r], 0, j)),
            pl.BlockSpec((None, D, tf), lambda r, j, te: (te[r], 0, j)),
            pl.BlockSpec((None, tf, D), lambda r, j, te: (te[r], j, 0)),
        ],
        out_specs=pl.BlockSpec((tm, D), lambda r, j, te: (r, 0)),
        scratch_shapes=[
            pltpu.VMEM((2, tm, D), F32),
            pltpu.VMEM((tm, D), BF16),
            pltpu.VMEM((tm, D), F32),
            pltpu.SemaphoreType.DMA((2,)),
        ],
    )
    return pl.pallas_call(
        functools.partial(_moe_kernel, tm=tm, nj=nj),
        grid_spec=grid_spec,
        out_shape=jax.ShapeDtypeStruct((P, D), F32),
        compiler_params=_cparams(("arbitrary", "arbitrary")),
        name="moe",
    )(tile_expert, src, src, h, wg, wu, wd)


def _combine_kernel(start_ref, base_ref, x_ref, route_ref, mod_ref, gfin_ref, ys_ref, o_ref, sbuf, sem,
                    *, d, tb, final_norm):
    i = pl.program_id(0)
    slot = i % 2
    rows = tb + SUBLANES
    kdim = sbuf.shape[2]

    def slab_copy(tile, e, s):
        start = pl.multiple_of(start_ref[tile * N_EXPERTS + e], SUBLANES)
        return pltpu.make_async_copy(ys_ref.at[pl.ds(start, rows)], sbuf.at[s, e, pl.ds(0, rows)], sem.at[s])

    @pl.when(i == 0)
    def _():
        for s in range(2):
            for e in range(N_EXPERTS):
                sbuf[s, e, rows:, :] = jnp.zeros((kdim - rows, d), F32)
        for e in range(N_EXPERTS):
            slab_copy(0, e, 0).start()

    @pl.when(i + 1 < pl.num_programs(0))
    def _():
        for e in range(N_EXPERTS):
            slab_copy(i + 1, e, 1 - slot).start()

    for e in range(N_EXPERTS):
        slab_copy(i, e, slot).wait()

    route = route_ref[...]
    e1, e2, w1, w2, r1, r2 = (route[:, k:k + 1] for k in range(6))
    kcol = lax.broadcasted_iota(jnp.int32, (tb, kdim), 1).astype(F32)
    f = jnp.zeros((tb, d), F32)
    for e in range(N_EXPERTS):
        hit1 = e1 == float(e)
        hit2 = e2 == float(e)
        loc = jnp.where(hit1, r1, r2) - base_ref[i * N_EXPERTS + e].astype(F32)
        pick = ((hit1 | hit2) & (loc == kcol)).astype(BF16)
        ye = jnp.dot(pick, sbuf[slot, e].astype(BF16), preferred_element_type=F32)
        f = f + jnp.where(hit1, w1, jnp.where(hit2, w2, 0.0)) * ye
    y = x_ref[...] + mod_ref[:, 5 * d:6 * d] * f
    if final_norm:
        y = y * lax.rsqrt(jnp.mean(y * y, axis=-1, keepdims=True) + EPS) * gfin_ref[...]
    o_ref[...] = y


def _combine(x, route, mod, g_final, ys, start, base, tb, row_of_tile, final_norm):
    N, D = x.shape
    grid_spec = pltpu.PrefetchScalarGridSpec(
        num_scalar_prefetch=2,
        grid=(N // tb,),
        in_specs=[
            pl.BlockSpec((tb, D), lambda i, st, bs: (i, 0)),
            pl.BlockSpec((tb, LANES), lambda i, st, bs: (i, 0)),
            pl.BlockSpec((None, 1, 6 * D), lambda i, st, bs: (row_of_tile(i, tb), 0, 0)),
            pl.BlockSpec((1, D), lambda i, st, bs: (0, 0)),
            pl.BlockSpec(memory_space=pl.ANY),
        ],
        out_specs=pl.BlockSpec((tb, D), lambda i, st, bs: (i, 0)),
        scratch_shapes=[pltpu.VMEM((2, N_EXPERTS, tb + LANES, D), F32), pltpu.SemaphoreType.DMA((2,))],
    )
    return pl.pallas_call(
        functools.partial(_combine_kernel, d=D, tb=tb, final_norm=final_norm),
        grid_spec=grid_spec,
        out_shape=jax.ShapeDtypeStruct((N, D), F32),
        compiler_params=_cparams(("arbitrary",)),
        name="combine",
    )(start, base, x, route, mod, g_final, ys)


def _final_kernel(x_ref, g_ref, o_ref):
    x = x_ref[...]
    o_ref[...] = x * lax.rsqrt(jnp.mean(x * x, axis=-1, keepdims=True) + EPS) * g_ref[...]


def _final_norm(x, g, tm):
    N, D = x.shape
    return pl.pallas_call(
        _final_kernel,
        grid=(N // tm,),
        in_specs=[pl.BlockSpec((tm, D), lambda i: (i, 0)), pl.BlockSpec((1, D), lambda i: (0, 0))],
        out_specs=pl.BlockSpec((tm, D), lambda i: (i, 0)),
        out_shape=jax.ShapeDtypeStruct((N, D), F32),
        compiler_params=_cparams(("arbitrary",)),
        name="final_norm",
    )(x, g)


def _grid_sincos(t, grid_w, d):
    rows = t // grid_w
    row = jnp.repeat(jnp.arange(rows, dtype=F32), grid_w)
    col = jnp.tile(jnp.arange(grid_w, dtype=F32), rows)
    quarter = d // 4
    omega = 1.0 / (10000.0 ** (jnp.arange(quarter, dtype=F32) / quarter))

    def emb(p):
        a = p[:, None] * omega[None, :]
        return jnp.concatenate([jnp.sin(a), jnp.cos(a)], axis=-1)

    return jnp.concatenate([emb(row), emb(col)], axis=-1)


def _tile_plan(t_prompt_total, t_seq):
    tm = 512
    while t_prompt_total % tm or t_seq % tm:
        tm //= 2
    return tm


def kernel(x_prompt, x_sample, state_C, state_n, state_m, c, c_ctx, w_ada, b_ada, g_mix, w_in, b_gate, g_mh,
           w_dw, b_dw, ln_g, ln_b, w_out, g_ffn, w_ff_gate, w_ff_up, w_ff_down, w_router, w_moe_gate, w_moe_up,
           w_moe_down, g_final):
    B, S, D = x_prompt.shape
    DB, DS, _ = x_sample.shape
    depth = w_ada.shape[0]
    H, Dh = M_HEADS, HEAD_DIM
    mw = H * Dh
    cw = w_dw.shape[2]
    n_gates = 4 * H
    NP, NS = B * S, DB * DS
    N = NP + NS
    grid_w = 64
    assert NP % DS == 0 or DS % NP == 0
    tm = _tile_plan(NP, DS)

    def row_of_tile(i, tile):
        start = i * tile
        return jnp.where(start < NP, 0, 1 + (start - NP) // DS)

    xs = x_sample + _grid_sincos(DS, grid_w, D)[None]
    x = jnp.concatenate([x_prompt.reshape(NP, D), xs.reshape(NS, D)], axis=0)

    n_cond = 1 + DB
    r_pad = -(-n_cond // 8) * 8
    cond = jnp.zeros((r_pad, D), F32).at[0].set(c_ctx).at[1:n_cond].set(c)
    mods = _ada(cond, w_ada, b_ada).reshape(depth, r_pad, 1, 6 * D)

    zero_c = jnp.zeros((B, 2, H, Dh, Dh), F32)
    zero_n = jnp.zeros((B, 2, H, 1, LANES), F32)
    neg_m = jnp.full((B, 2, H, 1, LANES), NEG, F32)

    new_c, new_n, new_m = [], [], []
    for l in range(depth):
        mod = mods[l]
        w_l = w_in[l]
        w_main = w_l[:, :4 * mw + 2 * cw].astype(BF16)
        w_gate = jnp.pad(w_l[:, 4 * mw + 2 * cw:], ((0, 0), (0, LANES - n_gates))).astype(BF16)
        bg = jnp.pad(b_gate[l], (0, LANES - n_gates)).reshape(1, LANES)
        proj, gates = _inproj(x, mod, g_mix[l].reshape(1, D), w_main, w_gate, bg, tm, row_of_tile)

        gr = gates[:, :n_gates].reshape(N, 4, H).transpose(2, 1, 0)
        gr = jnp.pad(gr, ((0, 0), (0, 4), (0, 0)))

        gmh = g_mh[l].reshape(1, mw)
        hm_p, c_p, n_p, m_p = _mlstm(proj, gr, gmh, zero_c, zero_n, neg_m, n_seq=B, t=S, row0=0)
        c_s = state_C[:, l]
        n_s = jnp.broadcast_to(state_n[:, l][:, :, :, None, :], (DB, 2, H, 1, Dh))
        m_s = jnp.broadcast_to(state_m[:, l][:, :, :, None, None], (DB, 2, H, 1, LANES))
        hm_s, _, _, _ = _mlstm(proj, gr, gmh, c_s, n_s, m_s, n_seq=DB, t=DS, row0=NP)
        new_c.append(c_p)
        new_n.append(n_p[:, :, :, 0, :])
        new_m.append(m_p[:, :, :, 0, 0])

        wdw = jnp.pad(w_dw[l], ((0, 1), (0, 0)))
        conv_args = (wdw, b_dw[l].reshape(1, cw), ln_g[l].reshape(1, cw), ln_b[l].reshape(1, cw))
        u_p = _conv(proj, *conv_args, n_seq=B, t=S, row0=0, cw=cw)
        u_s = _conv(proj, *conv_args, n_seq=DB, t=DS, row0=NP, cw=cw)

        x = _outproj(hm_p, u_p, hm_s, u_s, x, mod, w_out[l].astype(BF16), tm, row_of_tile)

        gf = g_ffn[l].reshape(1, D)
        if l % 2 == 0:
            e = l // 2
            ff = w_ff_gate.shape[2]
            tf = ff // 2 if (ff // 2) % LANES == 0 else ff
            x = _ffn(x, mod, gf, w_ff_gate[e].astype(BF16), w_ff_up[e].astype(BF16),
                     w_ff_down[e].astype(BF16), tm, tf, row_of_tile)
        else:
            e = l // 2
            wr = jnp.pad(w_router[e], ((0, 0), (0, LANES - N_EXPERTS))).astype(BF16)
            tb = min(tm, COMBINE_ROWS)
            h, route, cnt, carry = _router(x, mod, gf, wr, tb, row_of_tile)
            tme = tm
            counts = cnt[0, :N_EXPERTS].astype(jnp.int32)
            padded = ((counts + tme - 1) // tme) * tme
            ends = jnp.cumsum(padded)
            offs = ends - padded
            e1 = route[:, 0].astype(jnp.int32)
            e2 = route[:, 1].astype(jnp.int32)
            pos1 = offs[e1] + route[:, 4].astype(jnp.int32)
            pos2 = offs[e2] + route[:, 5].astype(jnp.int32)
            n_tiles = (2 * N) // tme + N_EXPERTS
            n_rows = n_tiles * tme
            tok = jnp.arange(N, dtype=jnp.int32)
            src = jnp.zeros((n_rows,), jnp.int32).at[pos1].set(tok).at[pos2].set(tok)
            starts = jnp.arange(n_tiles, dtype=jnp.int32) * tme
            tile_expert = jnp.minimum(jnp.sum(starts[:, None] >= ends[None, :], axis=1), N_EXPERTS - 1)
            ffe = w_moe_gate.shape[3]
            tfe = ffe // 2 if (ffe // 2) % LANES == 0 else ffe
            ys = _moe(h, src, tile_expert.astype(jnp.int32), w_moe_gate[e].astype(BF16),
                      w_moe_up[e].astype(BF16), w_moe_down[e].astype(BF16), tme, tfe)
            first = offs[None, :] + carry[:, 0, :N_EXPERTS].astype(jnp.int32)
            start = jnp.minimum((first // SUBLANES) * SUBLANES, n_rows - (tb + SUBLANES))
            base = start - offs[None, :]
            x = _combine(x, route, mod, g_final.reshape(1, D), ys, start.reshape(-1), base.reshape(-1), tb,
                         row_of_tile, final_norm=(l == depth - 1))

    if (depth - 1) % 2 == 0:
        x = _final_norm(x, g_final.reshape(1, D), tm)

    y_prompt = x[:NP].reshape(B, S, D)
    y_sample = x[NP:].reshape(DB, DS, D)
    new_state_c = jnp.stack(new_c, axis=1)
    new_state_n = jnp.stack(new_n, axis=1)
    new_state_m = jnp.stack(new_m, axis=1)
    return (y_prompt, y_sample, new_state_c, new_state_n, new_state_m)
```

```python
import functools

import jax
import jax.numpy as jnp
from jax import lax
from jax.experimental import pallas as pl
from jax.experimental.pallas import tpu as pltpu

F32 = jnp.float32
BF16 = jnp.bfloat16
EPS = 1e-6
NEG = -1e30

M_HEADS = 4
HEAD_DIM = 128
CHUNK = 128
CONV_W = 31
CONV_HALO = 16
MLSTM_UNROLL = 4
CONV_ROWS = 32
COMBINE_ROWS = 256
N_EXPERTS = 8
LANES = 128
SUBLANES = 8
VMEM_LIMIT = 56 * 1024 * 1024


def _cparams(sem):
    return pltpu.CompilerParams(dimension_semantics=sem, vmem_limit_bytes=VMEM_LIMIT)


def _silu(a):
    return a * jax.nn.sigmoid(a)


def _rms_mod(x, g, scale, shift):
    y = x * lax.rsqrt(jnp.mean(x * x, axis=-1, keepdims=True) + EPS) * g
    return y * (1.0 + scale) + shift


def _ada_kernel(c_ref, w_ref, b_ref, o_ref):
    s = _silu(c_ref[...]).astype(BF16)
    o_ref[...] = jnp.dot(s, w_ref[...].astype(BF16), preferred_element_type=F32) + b_ref[...]


def _ada(cond, w_ada, b_ada):
    R, D = cond.shape
    L, _, D6 = w_ada.shape
    tn = 1536
    return pl.pallas_call(
        _ada_kernel,
        grid=(L, D6 // tn),
        in_specs=[
            pl.BlockSpec((R, D), lambda l, j: (0, 0)),
            pl.BlockSpec((None, D, tn), lambda l, j: (l, 0, j)),
            pl.BlockSpec((None, 1, tn), lambda l, j: (l, 0, j)),
        ],
        out_specs=pl.BlockSpec((None, R, tn), lambda l, j: (l, 0, j)),
        out_shape=jax.ShapeDtypeStruct((L, R, D6), F32),
        compiler_params=_cparams(("arbitrary", "arbitrary")),
        name="ada",
    )(cond, w_ada, b_ada.reshape(L, 1, D6))


def _inproj_kernel(x_ref, mod_ref, g_ref, w_ref, wg_ref, bg_ref, proj_ref, gates_ref, *, d, k_scale):
    h = _rms_mod(x_ref[...], g_ref[...], mod_ref[:, d:2 * d], mod_ref[:, 0:d]).astype(BF16)
    nw = w_ref.shape[1]
    cw = 512
    for j in range(nw // cw):
        r = jnp.dot(h, w_ref[:, j * cw:(j + 1) * cw], preferred_element_type=F32)
        if j == 1:
            r = r * k_scale
        proj_ref[:, j * cw:(j + 1) * cw] = r.astype(BF16)
    gates_ref[...] = jnp.dot(h, wg_ref[...], preferred_element_type=F32) + bg_ref[...]


def _inproj(x, mod, g, w_main, w_gate, b_gate, tm, row_of_tile):
    N, D = x.shape
    nw = w_main.shape[1]
    return pl.pallas_call(
        functools.partial(_inproj_kernel, d=D, k_scale=HEAD_DIM ** -0.5),
        grid=(N // tm,),
        in_specs=[
            pl.BlockSpec((tm, D), lambda i: (i, 0)),
            pl.BlockSpec((None, 1, 6 * D), lambda i: (row_of_tile(i, tm), 0, 0)),
            pl.BlockSpec((1, D), lambda i: (0, 0)),
            pl.BlockSpec((D, nw), lambda i: (0, 0)),
            pl.BlockSpec((D, LANES), lambda i: (0, 0)),
            pl.BlockSpec((1, LANES), lambda i: (0, 0)),
        ],
        out_specs=[
            pl.BlockSpec((tm, nw), lambda i: (i, 0)),
            pl.BlockSpec((tm, LANES), lambda i: (i, 0)),
        ],
        out_shape=[
            jax.ShapeDtypeStruct((N, nw), BF16),
            jax.ShapeDtypeStruct((N, LANES), F32),
        ],
        compiler_params=_cparams(("arbitrary",)),
        name="inproj",
    )(x, mod, g, w_main, w_gate, b_gate)


def _cummax_lanes(x, reverse):
    n = x.shape[1]
    lane = lax.broadcasted_iota(jnp.int32, x.shape, 1)
    sh = 1
    while sh < n:
        if reverse:
            x = jnp.maximum(x, jnp.where(lane < n - sh, pltpu.roll(x, n - sh, axis=1), -jnp.inf))
        else:
            x = jnp.maximum(x, jnp.where(lane >= sh, pltpu.roll(x, sh, axis=1), -jnp.inf))
        sh *= 2
    return x


def _mlstm_chunk(qc, kc, v_aug, a_b, e_b, wi_b, ws_b, g_row, dec, mask, c_ref, n_ref):
    L = CHUNK
    p = jnp.exp(jnp.where(mask, a_b + g_row, NEG))
    s = lax.dot_general(qc, kc, (((1,), (1,)), ((), ())), preferred_element_type=F32) * p
    r_intra = jnp.dot(s.astype(BF16), v_aug, preferred_element_type=F32)
    cn = jnp.concatenate([c_ref[...], n_ref[...]], axis=1)
    r_inter = jnp.dot(qc, cn.astype(BF16), preferred_element_type=F32)
    num = r_intra[:, :L] + wi_b * r_inter[:, :L]
    qn = r_intra[:, L:] + wi_b * r_inter[:, L:]
    kw = (kc.astype(F32) * ws_b).astype(BF16)
    r_state = lax.dot_general(kw, v_aug, (((0,), (0,)), ((), ())), preferred_element_type=F32)
    c_ref[...] = dec * cn[:, :L] + r_state[:, :L]
    n_ref[...] = dec * cn[:, L:] + r_state[:, L:]
    return num / jnp.maximum(jnp.abs(qn), e_b)


def _mlstm_kernel(q_ref, k_ref, v_ref, o_ref, gq_ref, gmh_ref, c0_ref, n0_ref, m0_ref,
                  hm_ref, ct_ref, nt_ref, mt_ref, hf_ref, hb_ref, nb_ref, rows_ref, g_ref, dec_ref, fac_ref,
                  *, nc, unroll):
    L = CHUNK
    row = lax.broadcasted_iota(jnp.int32, (L, L), 0)
    col = lax.broadcasted_iota(jnp.int32, (L, L), 1)
    masks = (col <= row, col >= row)
    zpad = jnp.zeros((L - SUBLANES, L), F32)
    ones = jnp.ones((L, L), BF16)

    ct_ref[...] = c0_ref[...]
    for d in range(2):
        nb_ref[d] = jnp.broadcast_to(n0_ref[d], (L, L)).T
        gi = gq_ref[2 * d]
        lf = jax.nn.log_sigmoid(gq_ref[2 * d + 1])
        cum = masks[1 - d].astype(F32)
        b = jnp.dot(lf, cum, preferred_element_type=F32, precision=lax.Precision.HIGHEST)
        btot = jnp.sum(lf, axis=1, keepdims=True)
        g = gi - b
        cg = _cummax_lanes(g, d == 1)
        w_log = btot - b + gi
        w_max = jnp.max(w_log, axis=1, keepdims=True)
        m = m0_ref[d][:, 0:1]
        m_in, m_out = [None] * nc, [None] * nc
        for c in (range(nc) if d == 0 else reversed(range(nc))):
            m_in[c] = m
            m = jnp.maximum(btot[c:c + 1, :] + m, w_max[c:c + 1, :])
            m_out[c] = m
        mt_ref[d] = jnp.broadcast_to(m, (1, LANES))
        m_in = jnp.concatenate(m_in, axis=0)
        m_out = jnp.concatenate(m_out, axis=0)
        a = -jnp.maximum(m_in, cg)
        rows_ref[4 * d + 0] = a
        rows_ref[4 * d + 1] = jnp.exp(a - b)
        rows_ref[4 * d + 2] = jnp.exp(m_in + a)
        rows_ref[4 * d + 3] = jnp.exp(w_log - m_out)
        g_ref[d] = g
        dec_ref[d] = jnp.broadcast_to(jnp.exp(btot + m_in - m_out), (nc, LANES))

    def columns(c, _):
        for d in range(2):
            x8 = jnp.concatenate([rows_ref[4 * d + k, pl.ds(c, 1), :] for k in range(4)]
                                 + [zpad[0:SUBLANES - 4, :]], axis=0)
            xt = jnp.concatenate([x8, zpad], axis=0).T
            for k in range(4):
                fac_ref[4 * d + k, c] = jnp.broadcast_to(xt[:, k:k + 1], (L, L))
        return 0

    lax.fori_loop(0, nc, columns, 0, unroll=unroll)

    def body(i, _):
        for d, h_ref in ((0, hf_ref), (1, hb_ref)):
            c = i if d == 0 else nc - 1 - i
            sl = pl.ds(pl.multiple_of(c * L, L), L)
            v_aug = jnp.concatenate([v_ref[sl, :], ones], axis=1)
            h_ref[sl, :] = _mlstm_chunk(
                q_ref[sl, :], k_ref[sl, :], v_aug, fac_ref[4 * d + 0, c], fac_ref[4 * d + 1, c],
                fac_ref[4 * d + 2, c], fac_ref[4 * d + 3, c], g_ref[d, pl.ds(c, 1), :], dec_ref[d, pl.ds(c, 1), :],
                masks[d], ct_ref.at[d], nb_ref.at[d])
        return 0

    lax.fori_loop(0, nc, body, 0, unroll=unroll)
    for d in range(2):
        nt_ref[d] = nb_ref[d].T[0:1, :]

    gmh = gmh_ref[...]

    def fin(c, _):
        sl = pl.ds(pl.multiple_of(c * L, L), L)
        hm = hf_ref[sl, :] + hb_ref[sl, :]
        mu = jnp.mean(hm, axis=-1, keepdims=True)
        var = jnp.mean(jnp.square(hm - mu), axis=-1, keepdims=True)
        y = (hm - mu) * lax.rsqrt(var + EPS) * gmh
        hm_ref[sl, :] = (jax.nn.sigmoid(o_ref[sl, :].astype(F32)) * y).astype(BF16)
        return 0

    lax.fori_loop(0, nc, fin, 0, unroll=unroll)


def _mlstm(proj, gq, g_mh, c0, n0, m0, *, n_seq, t, row0):
    H, Dh = M_HEADS, HEAD_DIM
    b0 = row0 // t
    assert row0 % t == 0 and t % CHUNK == 0
    nc = t // CHUNK

    def pspec(part):
        return pl.BlockSpec((t, Dh), lambda b, h: (b0 + b, part * H + h))

    st_spec_c = pl.BlockSpec((None, 2, None, Dh, Dh), lambda b, h: (b, 0, h, 0, 0))
    st_spec_v = pl.BlockSpec((None, 2, None, 1, LANES), lambda b, h: (b, 0, h, 0, 0))
    return pl.pallas_call(
        functools.partial(_mlstm_kernel, nc=nc, unroll=min(nc, MLSTM_UNROLL)),
        grid=(n_seq, H),
        in_specs=[pspec(0), pspec(1), pspec(2), pspec(3),
                  pl.BlockSpec((None, 4, None, nc, CHUNK), lambda b, h: (h, 0, b, 0, 0)),
                  pl.BlockSpec((1, Dh), lambda b, h: (0, h)),
                  st_spec_c, st_spec_v, st_spec_v],
        out_specs=[
            pl.BlockSpec((t, Dh), lambda b, h: (b, h)),
            st_spec_c, st_spec_v, st_spec_v,
        ],
        out_shape=[
            jax.ShapeDtypeStruct((n_seq * t, H * Dh), BF16),
            jax.ShapeDtypeStruct((n_seq, 2, H, Dh, Dh), F32),
            jax.ShapeDtypeStruct((n_seq, 2, H, 1, LANES), F32),
            jax.ShapeDtypeStruct((n_seq, 2, H, 1, LANES), F32),
        ],
        scratch_shapes=[
            pltpu.VMEM((t, Dh), F32),
            pltpu.VMEM((t, Dh), F32),
            pltpu.VMEM((2, Dh, LANES), F32),
            pltpu.VMEM((8, nc, CHUNK), F32),
            pltpu.VMEM((2, nc, CHUNK), F32),
            pltpu.VMEM((2, nc, LANES), F32),
            pltpu.VMEM((8, nc, CHUNK, LANES), F32),
        ],
        compiler_params=_cparams(("arbitrary", "arbitrary")),
        name="mlstm",
    )(proj, proj, proj, proj, gq, g_mh, c0, n0, m0)


def _conv_kernel(ga_ref, gb_ref, w_ref, bdw_ref, lng_ref, lnb_ref, u_ref, pad_ref, *, t, tt):
    cw = pad_ref.shape[1]
    zeros = jnp.zeros((CONV_HALO, cw), F32)
    pad_ref[0:CONV_HALO, :] = zeros
    pad_ref[CONV_HALO + t:2 * CONV_HALO + t, :] = zeros

    def fill(i, _):
        sl = pl.ds(pl.multiple_of(i * tt, tt), tt)
        glu = ga_ref[sl, :].astype(F32) * jax.nn.sigmoid(gb_ref[sl, :].astype(F32))
        pad_ref[pl.ds(pl.multiple_of(CONV_HALO + i * tt, 8), tt), :] = glu
        return 0

    lax.fori_loop(0, t // tt, fill, 0)
    shift0 = CONV_HALO - CONV_W // 2
    rows = tt + 2 * CONV_HALO

    def conv(i, _):
        t0 = pl.multiple_of(i * tt, tt)
        accs = []
        for lg in range(cw // LANES):
            ls = slice(lg * LANES, (lg + 1) * LANES)
            win = pad_ref[pl.ds(t0, rows), ls]
            acc = jnp.broadcast_to(bdw_ref[:, ls], (tt, LANES))
            for r in range(8):
                wr = win if r == 0 else pltpu.roll(win, rows - r, axis=0)
                for a in range(4):
                    j = 8 * a + r - shift0
                    if 0 <= j < CONV_W:
                        acc = acc + wr[8 * a:8 * a + tt, :] * w_ref[j:j + 1, ls]
            accs.append(acc)
        acc = jnp.concatenate(accs, axis=1)
        mu = jnp.mean(acc, axis=-1, keepdims=True)
        var = jnp.mean(jnp.square(acc - mu), axis=-1, keepdims=True)
        y = (acc - mu) * lax.rsqrt(var + EPS) * lng_ref[...] + lnb_ref[...]
        u_ref[pl.ds(t0, tt), :] = _silu(y).astype(BF16)
        return 0

    lax.fori_loop(0, t // tt, conv, 0)


def _conv(proj, w_dw, b_dw, ln_g, ln_b, *, n_seq, t, row0, cw):
    b0 = row0 // t
    assert row0 % t == 0
    ncol = proj.shape[1] // cw
    return pl.pallas_call(
        functools.partial(_conv_kernel, t=t, tt=CONV_ROWS),
        grid=(n_seq,),
        in_specs=[
            pl.BlockSpec((t, cw), lambda b: (b0 + b, ncol - 2)),
            pl.BlockSpec((t, cw), lambda b: (b0 + b, ncol - 1)),
            pl.BlockSpec((CONV_W + 1, cw), lambda b: (0, 0)),
            pl.BlockSpec((1, cw), lambda b: (0, 0)),
            pl.BlockSpec((1, cw), lambda b: (0, 0)),
            pl.BlockSpec((1, cw), lambda b: (0, 0)),
        ],
        out_specs=pl.BlockSpec((t, cw), lambda b: (b, 0)),
        out_shape=jax.ShapeDtypeStruct((n_seq * t, cw), BF16),
        scratch_shapes=[pltpu.VMEM((t + 2 * CONV_HALO, cw), F32)],
        compiler_params=_cparams(("arbitrary",)),
        name="conv",
    )(proj, proj, w_dw, b_dw, ln_g, ln_b)


def _outproj_kernel(hmp_ref, up_ref, hms_ref, us_ref, x_ref, mod_ref, w_ref, o_ref, *, d, n_prompt_tiles):
    mw = hmp_ref.shape[1]

    def project(hm_ref, u_ref):
        acc = jnp.dot(hm_ref[...], w_ref[0:mw, :], preferred_element_type=F32)
        acc = acc + jnp.dot(u_ref[...], w_ref[mw:, :], preferred_element_type=F32)
        o_ref[...] = x_ref[...] + mod_ref[:, 2 * d:3 * d] * acc

    is_prompt = pl.program_id(0) < n_prompt_tiles
    pl.when(is_prompt)(lambda: project(hmp_ref, up_ref))
    pl.when(jnp.logical_not(is_prompt))(lambda: project(hms_ref, us_ref))


def _outproj(hm_p, u_p, hm_s, u_s, x, mod, w_out, tm, row_of_tile):
    N, D = x.shape
    mw, cw = hm_p.shape[1], u_p.shape[1]
    npt = hm_p.shape[0] // tm
    return pl.pallas_call(
        functools.partial(_outproj_kernel, d=D, n_prompt_tiles=npt),
        grid=(N // tm,),
        in_specs=[
            pl.BlockSpec((tm, mw), lambda i: (jnp.minimum(i, npt - 1), 0)),
            pl.BlockSpec((tm, cw), lambda i: (jnp.minimum(i, npt - 1), 0)),
            pl.BlockSpec((tm, mw), lambda i: (jnp.maximum(i - npt, 0), 0)),
            pl.BlockSpec((tm, cw), lambda i: (jnp.maximum(i - npt, 0), 0)),
            pl.BlockSpec((tm, D), lambda i: (i, 0)),
            pl.BlockSpec((None, 1, 6 * D), lambda i: (row_of_tile(i, tm), 0, 0)),
            pl.BlockSpec((mw + cw, D), lambda i: (0, 0)),
        ],
        out_specs=pl.BlockSpec((tm, D), lambda i: (i, 0)),
        out_shape=jax.ShapeDtypeStruct((N, D), F32),
        compiler_params=_cparams(("arbitrary",)),
        name="outproj",
    )(hm_p, u_p, hm_s, u_s, x, mod, w_out)


def _ffn_kernel(x_ref, mod_ref, g_ref, wg_ref, wu_ref, wd_ref, o_ref, h_scr, acc_scr, *, d):
    j = pl.program_id(1)

    @pl.when(j == 0)
    def _():
        h_scr[...] = _rms_mod(x_ref[...], g_ref[...], mod_ref[:, 4 * d:5 * d], mod_ref[:, 3 * d:4 * d]).astype(BF16)
        acc_scr[...] = jnp.zeros_like(acc_scr)

    h = h_scr[...]
    a = jnp.dot(h, wg_ref[...], preferred_element_type=F32)
    b = jnp.dot(h, wu_ref[...], preferred_element_type=F32)
    acc_scr[...] += jnp.dot((_silu(a) * b).astype(BF16), wd_ref[...], preferred_element_type=F32)

    @pl.when(j == pl.num_programs(1) - 1)
    def _():
        o_ref[...] = x_ref[...] + mod_ref[:, 5 * d:6 * d] * acc_scr[...]


def _ffn(x, mod, g, wg, wu, wd, tm, tf, row_of_tile):
    N, D = x.shape
    ff = wg.shape[1]
    return pl.pallas_call(
        functools.partial(_ffn_kernel, d=D),
        grid=(N // tm, ff // tf),
        in_specs=[
            pl.BlockSpec((tm, D), lambda i, j: (i, 0)),
            pl.BlockSpec((None, 1, 6 * D), lambda i, j: (row_of_tile(i, tm), 0, 0)),
            pl.BlockSpec((1, D), lambda i, j: (0, 0)),
            pl.BlockSpec((D, tf), lambda i, j: (0, j)),
            pl.BlockSpec((D, tf), lambda i, j: (0, j)),
            pl.BlockSpec((tf, D), lambda i, j: (j, 0)),
        ],
        out_specs=pl.BlockSpec((tm, D), lambda i, j: (i, 0)),
        out_shape=jax.ShapeDtypeStruct((N, D), F32),
        scratch_shapes=[pltpu.VMEM((tm, D), BF16), pltpu.VMEM((tm, D), F32)],
        compiler_params=_cparams(("arbitrary", "arbitrary")),
        name="ffn",
    )(x, mod, g, wg, wu, wd)


def _router_kernel(x_ref, mod_ref, g_ref, wr_ref, h_ref, route_ref, cnt_ref, carry_ref, carry, *, d):
    i = pl.program_id(0)

    @pl.when(i == 0)
    def _():
        carry[...] = jnp.zeros_like(carry)

    h = _rms_mod(x_ref[...], g_ref[...], mod_ref[:, 4 * d:5 * d], mod_ref[:, 3 * d:4 * d])
    for k in range(d // LANES):
        h_ref[:, k, :] = h[:, k * LANES:(k + 1) * LANES]
    tm = h.shape[0]
    logits = jnp.dot(h.astype(BF16), wr_ref[...], preferred_element_type=F32)
    lane = lax.broadcasted_iota(jnp.int32, (tm, LANES), 1)
    lg = jnp.where(lane < N_EXPERTS, logits, -jnp.inf)
    m1 = jnp.max(lg, axis=1, keepdims=True)
    i1 = jnp.min(jnp.where(lg == m1, lane, LANES), axis=1, keepdims=True)
    lg2 = jnp.where(lane == i1, -jnp.inf, lg)
    m2 = jnp.max(lg2, axis=1, keepdims=True)
    i2 = jnp.min(jnp.where(lg2 == m2, lane, LANES), axis=1, keepdims=True)
    e = jnp.exp(m2 - m1)
    w1 = 1.0 / (1.0 + e)
    w2 = e / (1.0 + e)
    sel1 = lane == i1
    sel2 = lane == i2
    onehot = (sel1 | sel2).astype(BF16)
    rr = lax.broadcasted_iota(jnp.int32, (tm, tm), 0)
    cc = lax.broadcasted_iota(jnp.int32, (tm, tm), 1)
    before = (cc < rr).astype(BF16)
    carry_ref[...] = jnp.broadcast_to(carry[...], carry_ref.shape)
    rank = jnp.dot(before, onehot, preferred_element_type=F32) + carry[...]
    r1 = jnp.sum(jnp.where(sel1, rank, 0.0), axis=1, keepdims=True)
    r2 = jnp.sum(jnp.where(sel2, rank, 0.0), axis=1, keepdims=True)
    carry[...] += jnp.sum(onehot.astype(F32), axis=0, keepdims=True)
    vals = (i1.astype(F32), i2.astype(F32), w1, w2, r1, r2)
    out = jnp.zeros((tm, LANES), F32)
    for k, v in enumerate(vals):
        out = jnp.where(lane == k, v, out)
    route_ref[...] = out
    cnt_ref[...] = jnp.broadcast_to(carry[...], cnt_ref.shape)


def _router(x, mod, g, w_router, tm, row_of_tile):
    N, D = x.shape
    assert D == SUBLANES * LANES
    return pl.pallas_call(
        functools.partial(_router_kernel, d=D),
        grid=(N // tm,),
        in_specs=[
            pl.BlockSpec((tm, D), lambda i: (i, 0)),
            pl.BlockSpec((None, 1, 6 * D), lambda i: (row_of_tile(i, tm), 0, 0)),
            pl.BlockSpec((1, D), lambda i: (0, 0)),
            pl.BlockSpec((D, LANES), lambda i: (0, 0)),
        ],
        out_specs=[
            pl.BlockSpec((tm, SUBLANES, LANES), lambda i: (i, 0, 0)),
            pl.BlockSpec((tm, LANES), lambda i: (i, 0)),
            pl.BlockSpec((8, LANES), lambda i: (0, 0)),
            pl.BlockSpec((None, 8, LANES), lambda i: (i, 0, 0)),
        ],
        out_shape=[
            jax.ShapeDtypeStruct((N, SUBLANES, LANES), F32),
            jax.ShapeDtypeStruct((N, LANES), F32),
            jax.ShapeDtypeStruct((8, LANES), F32),
            jax.ShapeDtypeStruct((N // tm, 8, LANES), F32),
        ],
        scratch_shapes=[pltpu.VMEM((1, LANES), F32)],
        compiler_params=_cparams(("arbitrary",)),
        name="router",
    )(x, mod, g, w_router)


def _moe_kernel(te_ref, src0_ref, srcn_ref, h_ref, wg_ref, wu_ref, wd_ref, o_ref,
                xg, xb_scr, acc_scr, sem, *, tm, nj):
    del te_ref
    r = pl.program_id(0)
    j = pl.program_id(1)
    slot = r % 2
    other = 1 - slot
    ch = tm // nj

    def row_copy(idx_ref, row, s):
        return pltpu.make_async_copy(h_ref.at[idx_ref[row]], xg.at[s, row], sem.at[s])

    def drain(s):
        def wait_one(t, c):
            row_copy(src0_ref, 0, s).wait()
            return c
        lax.fori_loop(0, tm, wait_one, 0)

    @pl.when((r == 0) & (j == 0))
    def _():
        def issue(t, c):
            row_copy(src0_ref, t, 0).start()
            return c
        lax.fori_loop(0, tm, issue, 0)

    @pl.when(j == 0)
    def _():
        drain(slot)
        for k in range(xg.shape[2]):
            xb_scr[:, k * LANES:(k + 1) * LANES] = xg[slot, :, k, :].astype(BF16)
        acc_scr[...] = jnp.zeros_like(acc_scr)

    base = j * ch
    for t in range(ch):
        row_copy(srcn_ref, base + t, other).start()

    h = xb_scr[...]
    a = jnp.dot(h, wg_ref[...], preferred_element_type=F32)
    b = jnp.dot(h, wu_ref[...], preferred_element_type=F32)
    acc_scr[...] += jnp.dot((_silu(a) * b).astype(BF16), wd_ref[...], preferred_element_type=F32)

    @pl.when(j == nj - 1)
    def _():
        o_ref[...] = acc_scr[...]

    @pl.when((r == pl.num_programs(0) - 1) & (j == nj - 1))
    def _():
        drain(other)


def _moe(h, src, tile_expert, wg, wu, wd, tm, tf):
    P = src.shape[0]
    D = h.shape[1] * h.shape[2]
    ff = wg.shape[2]
    nj = ff // tf
    nr = P // tm
    assert tm % nj == 0
    grid_spec = pltpu.PrefetchScalarGridSpec(
        num_scalar_prefetch=1,
        grid=(nr, nj),
        in_specs=[
            pl.BlockSpec((tm,), lambda r, j, te: (0,), memory_space=pltpu.SMEM),
            pl.BlockSpec((tm,), lambda r, j, te: (jnp.minimum(r + 1, nr - 1),), memory_space=pltpu.SMEM),
            pl.BlockSpec(memory_space=pl.ANY),
            pl.BlockSpec((None, D, tf), lambda r, j, te: (te[r], 0, j)),
            pl.BlockSpec((None, D, tf), lambda r, j, te: (te[r], 0, j)),
            pl.BlockSpec((None, tf, D), lambda r, j, te: (te[r], j, 0)),
        ],
        out_specs=pl.BlockSpec((tm, D), lambda r, j, te: (r, 0)),
        scratch_shapes=[
            pltpu.VMEM((2, tm, D // LANES, LANES), F32),
            pltpu.VMEM((tm, D), BF16),
            pltpu.VMEM((tm, D), F32),
            pltpu.SemaphoreType.DMA((2,)),
        ],
    )
    return pl.pallas_call(
        functools.partial(_moe_kernel, tm=tm, nj=nj),
        grid_spec=grid_spec,
        out_shape=jax.ShapeDtypeStruct((P, D), F32),
        compiler_params=_cparams(("arbitrary", "arbitrary")),
        name="moe",
    )(tile_expert, src, src, h, wg, wu, wd)


def _combine_kernel(start_ref, base_ref, x_ref, route_ref, mod_ref, gfin_ref, ys_ref, *rest,
                    d, tb, final_norm, n_prompt_tiles):
    *o_refs, sbuf, sem = rest
    i = pl.program_id(0)
    slot = i % 2
    rows = tb + SUBLANES
    kdim = sbuf.shape[2]

    def slab_copy(tile, e, s):
        start = pl.multiple_of(start_ref[tile * N_EXPERTS + e], SUBLANES)
        return pltpu.make_async_copy(ys_ref.at[pl.ds(start, rows)], sbuf.at[s, e, pl.ds(0, rows)], sem.at[s])

    @pl.when(i == 0)
    def _():
        for s in range(2):
            for e in range(N_EXPERTS):
                sbuf[s, e, rows:, :] = jnp.zeros((kdim - rows, d), F32)
        for e in range(N_EXPERTS):
            slab_copy(0, e, 0).start()

    @pl.when(i + 1 < pl.num_programs(0))
    def _():
        for e in range(N_EXPERTS):
            slab_copy(i + 1, e, 1 - slot).start()

    for e in range(N_EXPERTS):
        slab_copy(i, e, slot).wait()

    route = route_ref[...]
    e1, e2, w1, w2, r1, r2 = (route[:, k:k + 1] for k in range(6))
    kcol = lax.broadcasted_iota(jnp.int32, (tb, kdim), 1).astype(F32)
    f = jnp.zeros((tb, d), F32)
    for e in range(N_EXPERTS):
        hit1 = e1 == float(e)
        hit2 = e2 == float(e)
        loc = jnp.where(hit1, r1, r2) - base_ref[i * N_EXPERTS + e].astype(F32)
        pick = ((hit1 | hit2) & (loc == kcol)).astype(BF16)
        ye = jnp.dot(pick, sbuf[slot, e].astype(BF16), preferred_element_type=F32)
        f = f + jnp.where(hit1, w1, jnp.where(hit2, w2, 0.0)) * ye
    y = x_ref[...] + mod_ref[:, 5 * d:6 * d] * f
    if final_norm:
        y = y * lax.rsqrt(jnp.mean(y * y, axis=-1, keepdims=True) + EPS) * gfin_ref[...]
    if n_prompt_tiles is None:
        o_refs[0][...] = y
    else:

        def write_prompt():
            o_refs[0][...] = y

        def write_latent():
            o_refs[1][...] = y

        pl.when(i < n_prompt_tiles)(write_prompt)
        pl.when(i >= n_prompt_tiles)(write_latent)


def _combine(x, route, mod, g_final, ys, start, base, tb, row_of_tile, final_norm, n_prompt=None):
    N, D = x.shape
    if n_prompt is None:
        npt = None
        out_specs = pl.BlockSpec((tb, D), lambda i, st, bs: (i, 0))
        out_shape = jax.ShapeDtypeStruct((N, D), F32)
    else:
        npt = n_prompt // tb
        out_specs = [pl.BlockSpec((tb, D), lambda i, st, bs: (jnp.minimum(i, npt - 1), 0)),
                     pl.BlockSpec((tb, D), lambda i, st, bs: (jnp.maximum(i - npt, 0), 0))]
        out_shape = [jax.ShapeDtypeStruct((n_prompt, D), F32), jax.ShapeDtypeStruct((N - n_prompt, D), F32)]
    grid_spec = pltpu.PrefetchScalarGridSpec(
        num_scalar_prefetch=2,
        grid=(N // tb,),
        in_specs=[
            pl.BlockSpec((tb, D), lambda i, st, bs: (i, 0)),
            pl.BlockSpec((tb, LANES), lambda i, st, bs: (i, 0)),
            pl.BlockSpec((None, 1, 6 * D), lambda i, st, bs: (row_of_tile(i, tb), 0, 0)),
            pl.BlockSpec((1, D), lambda i, st, bs: (0, 0)),
            pl.BlockSpec(memory_space=pl.ANY),
        ],
        out_specs=out_specs,
        scratch_shapes=[pltpu.VMEM((2, N_EXPERTS, tb + LANES, D), F32), pltpu.SemaphoreType.DMA((2,))],
    )
    return pl.pallas_call(
        functools.partial(_combine_kernel, d=D, tb=tb, final_norm=final_norm, n_prompt_tiles=npt),
        grid_spec=grid_spec,
        out_shape=out_shape,
        compiler_params=_cparams(("arbitrary",)),
        name="combine",
    )(start, base, x, route, mod, g_final, ys)


def _final_kernel(x_ref, g_ref, o_ref):
    x = x_ref[...]
    o_ref[...] = x * lax.rsqrt(jnp.mean(x * x, axis=-1, keepdims=True) + EPS) * g_ref[...]


def _final_norm(x, g, tm):
    N, D = x.shape
    return pl.pallas_call(
        _final_kernel,
        grid=(N // tm,),
        in_specs=[pl.BlockSpec((tm, D), lambda i: (i, 0)), pl.BlockSpec((1, D), lambda i: (0, 0))],
        out_specs=pl.BlockSpec((tm, D), lambda i: (i, 0)),
        out_shape=jax.ShapeDtypeStruct((N, D), F32),
        compiler_params=_cparams(("arbitrary",)),
        name="final_norm",
    )(x, g)


def _grid_sincos(t, grid_w, d):
    rows = t // grid_w
    row = jnp.repeat(jnp.arange(rows, dtype=F32), grid_w)
    col = jnp.tile(jnp.arange(grid_w, dtype=F32), rows)
    quarter = d // 4
    omega = 1.0 / (10000.0 ** (jnp.arange(quarter, dtype=F32) / quarter))

    def emb(p):
        a = p[:, None] * omega[None, :]
        return jnp.concatenate([jnp.sin(a), jnp.cos(a)], axis=-1)

    return jnp.concatenate([emb(row), emb(col)], axis=-1)


def _tile_plan(t_prompt_total, t_seq):
    tm = 512
    while t_prompt_total % tm or t_seq % tm:
        tm //= 2
    return tm


def kernel(x_prompt, x_sample, state_C, state_n, state_m, c, c_ctx, w_ada, b_ada, g_mix, w_in, b_gate, g_mh,
           w_dw, b_dw, ln_g, ln_b, w_out, g_ffn, w_ff_gate, w_ff_up, w_ff_down, w_router, w_moe_gate, w_moe_up,
           w_moe_down, g_final):
    B, S, D = x_prompt.shape
    DB, DS, _ = x_sample.shape
    depth = w_ada.shape[0]
    H, Dh = M_HEADS, HEAD_DIM
    mw = H * Dh
    cw = w_dw.shape[2]
    n_gates = 4 * H
    NP, NS = B * S, DB * DS
    N = NP + NS
    grid_w = 64
    assert NP % DS == 0 or DS % NP == 0
    tm = _tile_plan(NP, DS)

    def row_of_tile(i, tile):
        start = i * tile
        return jnp.where(start < NP, 0, 1 + (start - NP) // DS)

    xs = x_sample + _grid_sincos(DS, grid_w, D)[None]
    x = jnp.concatenate([x_prompt.reshape(NP, D), xs.reshape(NS, D)], axis=0)

    n_cond = 1 + DB
    r_pad = -(-n_cond // 8) * 8
    cond = jnp.zeros((r_pad, D), F32).at[0].set(c_ctx).at[1:n_cond].set(c)
    mods = _ada(cond, w_ada, b_ada).reshape(depth, r_pad, 1, 6 * D)

    zero_c = jnp.zeros((B, 2, H, Dh, Dh), F32)
    zero_n = jnp.zeros((B, 2, H, 1, LANES), F32)
    neg_m = jnp.full((B, 2, H, 1, LANES), NEG, F32)

    new_c, new_n, new_m = [], [], []
    for l in range(depth):
        mod = mods[l]
        w_l = w_in[l]
        w_main = w_l[:, :4 * mw + 2 * cw].astype(BF16)
        w_gate = jnp.pad(w_l[:, 4 * mw + 2 * cw:], ((0, 0), (0, LANES - n_gates))).astype(BF16)
        bg = jnp.pad(b_gate[l], (0, LANES - n_gates)).reshape(1, LANES)
        proj, gates = _inproj(x, mod, g_mix[l].reshape(1, D), w_main, w_gate, bg, tm, row_of_tile)

        gq = gates[:, :n_gates].reshape(N, 4, H).transpose(2, 1, 0)
        gq_p = gq[:, :, :NP].reshape(H, 4, B, S // CHUNK, CHUNK)
        gq_s = gq[:, :, NP:].reshape(H, 4, DB, DS // CHUNK, CHUNK)

        gmh = g_mh[l].reshape(1, mw)
        hm_p, c_p, n_p, m_p = _mlstm(proj, gq_p, gmh, zero_c, zero_n, neg_m, n_seq=B, t=S, row0=0)
        c_s = state_C[:, l]
        n_s = jnp.broadcast_to(state_n[:, l][:, :, :, None, :], (DB, 2, H, 1, Dh))
        m_s = jnp.broadcast_to(state_m[:, l][:, :, :, None, None], (DB, 2, H, 1, LANES))
        hm_s, _, _, _ = _mlstm(proj, gq_s, gmh, c_s, n_s, m_s, n_seq=DB, t=DS, row0=NP)
        new_c.append(c_p)
        new_n.append(n_p[:, :, :, 0, :])
        new_m.append(m_p[:, :, :, 0, 0])

        wdw = jnp.pad(w_dw[l], ((0, 1), (0, 0)))
        conv_args = (wdw, b_dw[l].reshape(1, cw), ln_g[l].reshape(1, cw), ln_b[l].reshape(1, cw))
        u_p = _conv(proj, *conv_args, n_seq=B, t=S, row0=0, cw=cw)
        u_s = _conv(proj, *conv_args, n_seq=DB, t=DS, row0=NP, cw=cw)

        x = _outproj(hm_p, u_p, hm_s, u_s, x, mod, w_out[l].astype(BF16), tm, row_of_tile)

        gf = g_ffn[l].reshape(1, D)
        if l % 2 == 0:
            e = l // 2
            ff = w_ff_gate.shape[2]
            tf = ff // 2 if (ff // 2) % LANES == 0 else ff
            x = _ffn(x, mod, gf, w_ff_gate[e].astype(BF16), w_ff_up[e].astype(BF16),
                     w_ff_down[e].astype(BF16), tm, tf, row_of_tile)
        else:
            e = l // 2
            wr = jnp.pad(w_router[e], ((0, 0), (0, LANES - N_EXPERTS))).astype(BF16)
            tb = min(tm, COMBINE_ROWS)
            h, route, cnt, carry = _router(x, mod, gf, wr, tb, row_of_tile)
            tme = tm
            counts = cnt[0, :N_EXPERTS].astype(jnp.int32)
            padded = ((counts + tme - 1) // tme) * tme
            ends = jnp.cumsum(padded)
            offs = ends - padded
            e1 = route[:, 0].astype(jnp.int32)
            e2 = route[:, 1].astype(jnp.int32)
            pos1 = offs[e1] + route[:, 4].astype(jnp.int32)
            pos2 = offs[e2] + route[:, 5].astype(jnp.int32)
            n_tiles = (2 * N) // tme + N_EXPERTS
            n_rows = n_tiles * tme
            tok = jnp.arange(N, dtype=jnp.int32)
            src = jnp.zeros((n_rows,), jnp.int32).at[jnp.concatenate([pos1, pos2])].set(
                jnp.concatenate([tok, tok]), unique_indices=True)
            starts = jnp.arange(n_tiles, dtype=jnp.int32) * tme
            tile_expert = jnp.minimum(jnp.sum(starts[:, None] >= ends[None, :], axis=1), N_EXPERTS - 1)
            ffe = w_moe_gate.shape[3]
            tfe = ffe // 2 if (ffe // 2) % LANES == 0 else ffe
            ys = _moe(h, src, tile_expert.astype(jnp.int32), w_moe_gate[e].astype(BF16),
                      w_moe_up[e].astype(BF16), w_moe_down[e].astype(BF16), tme, tfe)
            first = offs[None, :] + carry[:, 0, :N_EXPERTS].astype(jnp.int32)
            start = jnp.minimum((first // SUBLANES) * SUBLANES, n_rows - (tb + SUBLANES))
            base = start - offs[None, :]
            last = l == depth - 1
            x = _combine(x, route, mod, g_final.reshape(1, D), ys, start.reshape(-1), base.reshape(-1), tb,
                         row_of_tile, final_norm=last, n_prompt=NP if last else None)

    if (depth - 1) % 2 == 0:
        x = _final_norm(x, g_final.reshape(1, D), tm)
        x = (x[:NP], x[NP:])

    y_prompt = x[0].reshape(B, S, D)
    y_sample = x[1].reshape(DB, DS, D)
    new_state_c = jnp.stack(new_c, axis=1)
    new_state_n = jnp.stack(new_n, axis=1)
    new_state_m = jnp.stack(new_m, axis=1)
    return (y_prompt, y_sample, new_state_c, new_state_n, new_state_m)
```

```python
import functools

import jax
import jax.numpy as jnp
from jax import lax
from jax.experimental import pallas as pl
from jax.experimental.pallas import tpu as pltpu

F32 = jnp.float32
BF16 = jnp.bfloat16
EPS = 1e-6
NEG = -1e30

M_HEADS = 4
HEAD_DIM = 128
CHUNK = 128
CONV_W = 31
CONV_HALO = 16
MLSTM_UNROLL = 4
CONV_ROWS = 64
COMBINE_ROWS = 256
MOE_COL_BLOCK = 256
N_EXPERTS = 8
LANES = 128
SUBLANES = 8
VMEM_LIMIT = 56 * 1024 * 1024


def _cparams(sem):
    return pltpu.CompilerParams(dimension_semantics=sem, vmem_limit_bytes=VMEM_LIMIT)


def _silu(a):
    return a * jax.nn.sigmoid(a)


def _rms_mod(x, g, scale, shift):
    y = x * lax.rsqrt(jnp.mean(x * x, axis=-1, keepdims=True) + EPS) * g
    return y * (1.0 + scale) + shift


def _ada_kernel(c_ref, w_ref, b_ref, o_ref):
    s = _silu(c_ref[...]).astype(BF16)
    o_ref[...] = jnp.dot(s, w_ref[...].astype(BF16), preferred_element_type=F32) + b_ref[...]


def _ada(cond, w_ada, b_ada):
    R, D = cond.shape
    L, _, D6 = w_ada.shape
    tn = 1536
    return pl.pallas_call(
        _ada_kernel,
        grid=(L, D6 // tn),
        in_specs=[
            pl.BlockSpec((R, D), lambda l, j: (0, 0)),
            pl.BlockSpec((None, D, tn), lambda l, j: (l, 0, j)),
            pl.BlockSpec((None, 1, tn), lambda l, j: (l, 0, j)),
        ],
        out_specs=pl.BlockSpec((None, R, tn), lambda l, j: (l, 0, j)),
        out_shape=jax.ShapeDtypeStruct((L, R, D6), F32),
        compiler_params=_cparams(("arbitrary", "arbitrary")),
        name="ada",
    )(cond, w_ada, b_ada.reshape(L, 1, D6))


def _inproj_kernel(x_ref, mod_ref, g_ref, w_ref, wg_ref, bg_ref, proj_ref, gates_ref, *, d, k_scale):
    h = _rms_mod(x_ref[...], g_ref[...], mod_ref[:, d:2 * d], mod_ref[:, 0:d]).astype(BF16)
    nw = w_ref.shape[1]
    cw = 512
    for j in range(nw // cw):
        r = jnp.dot(h, w_ref[:, j * cw:(j + 1) * cw], preferred_element_type=F32)
        if j == 1:
            r = r * k_scale
        proj_ref[:, j * cw:(j + 1) * cw] = r.astype(BF16)
    gates_ref[...] = jnp.dot(h, wg_ref[...], preferred_element_type=F32) + bg_ref[...]


def _inproj(x, mod, g, w_main, w_gate, b_gate, tm, row_of_tile):
    N, D = x.shape
    nw = w_main.shape[1]
    return pl.pallas_call(
        functools.partial(_inproj_kernel, d=D, k_scale=HEAD_DIM ** -0.5),
        grid=(N // tm,),
        in_specs=[
            pl.BlockSpec((tm, D), lambda i: (i, 0)),
            pl.BlockSpec((None, 1, 6 * D), lambda i: (row_of_tile(i, tm), 0, 0)),
            pl.BlockSpec((1, D), lambda i: (0, 0)),
            pl.BlockSpec((D, nw), lambda i: (0, 0)),
            pl.BlockSpec((D, LANES), lambda i: (0, 0)),
            pl.BlockSpec((1, LANES), lambda i: (0, 0)),
        ],
        out_specs=[
            pl.BlockSpec((tm, nw), lambda i: (i, 0)),
            pl.BlockSpec((tm, LANES), lambda i: (i, 0)),
        ],
        out_shape=[
            jax.ShapeDtypeStruct((N, nw), BF16),
            jax.ShapeDtypeStruct((N, LANES), F32),
        ],
        compiler_params=_cparams(("arbitrary",)),
        name="inproj",
    )(x, mod, g, w_main, w_gate, b_gate)


def _cummax_lanes(x, reverse):
    n = x.shape[1]
    lane = lax.broadcasted_iota(jnp.int32, x.shape, 1)
    sh = 1
    while sh < n:
        if reverse:
            x = jnp.maximum(x, jnp.where(lane < n - sh, pltpu.roll(x, n - sh, axis=1), -jnp.inf))
        else:
            x = jnp.maximum(x, jnp.where(lane >= sh, pltpu.roll(x, sh, axis=1), -jnp.inf))
        sh *= 2
    return x


def _mlstm_chunk(qc, kc, v_aug, a_b, e_b, wi_b, ws_b, g_row, dec, mask, c_ref, n_ref):
    L = CHUNK
    p = jnp.exp(jnp.where(mask, a_b + g_row, NEG))
    s = lax.dot_general(qc, kc, (((1,), (1,)), ((), ())), preferred_element_type=F32) * p
    r_intra = jnp.dot(s.astype(BF16), v_aug, preferred_element_type=F32)
    cn = jnp.concatenate([c_ref[...], n_ref[...]], axis=1)
    r_inter = jnp.dot(qc, cn.astype(BF16), preferred_element_type=F32)
    num = r_intra[:, :L] + wi_b * r_inter[:, :L]
    qn = r_intra[:, L:] + wi_b * r_inter[:, L:]
    kw = (kc.astype(F32) * ws_b).astype(BF16)
    r_state = lax.dot_general(kw, v_aug, (((0,), (0,)), ((), ())), preferred_element_type=F32)
    c_ref[...] = dec * cn[:, :L] + r_state[:, :L]
    n_ref[...] = dec * cn[:, L:] + r_state[:, L:]
    return num / jnp.maximum(jnp.abs(qn), e_b)


def _mlstm_kernel(q_ref, k_ref, v_ref, o_ref, gq_ref, gmh_ref, c0_ref, n0_ref, m0_ref,
                  hm_ref, ct_ref, nt_ref, mt_ref, hf_ref, hb_ref, nb_ref, rows_ref, g_ref, dec_ref, fac_ref,
                  *, nc, unroll):
    L = CHUNK
    row = lax.broadcasted_iota(jnp.int32, (L, L), 0)
    col = lax.broadcasted_iota(jnp.int32, (L, L), 1)
    masks = (col <= row, col >= row)
    zpad = jnp.zeros((L - SUBLANES, L), F32)
    ones = jnp.ones((L, L), BF16)

    ct_ref[...] = c0_ref[...]
    for d in range(2):
        nb_ref[d] = jnp.broadcast_to(n0_ref[d], (L, L)).T
        gi = gq_ref[2 * d]
        lf = jax.nn.log_sigmoid(gq_ref[2 * d + 1])
        cum = masks[1 - d].astype(F32)
        b = jnp.dot(lf, cum, preferred_element_type=F32, precision=lax.Precision.HIGHEST)
        btot = jnp.sum(lf, axis=1, keepdims=True)
        g = gi - b
        cg = _cummax_lanes(g, d == 1)
        w_log = btot - b + gi
        w_max = jnp.max(w_log, axis=1, keepdims=True)
        m = m0_ref[d][:, 0:1]
        m_in, m_out = [None] * nc, [None] * nc
        for c in (range(nc) if d == 0 else reversed(range(nc))):
            m_in[c] = m
            m = jnp.maximum(btot[c:c + 1, :] + m, w_max[c:c + 1, :])
            m_out[c] = m
        mt_ref[d] = jnp.broadcast_to(m, (1, LANES))
        m_in = jnp.concatenate(m_in, axis=0)
        m_out = jnp.concatenate(m_out, axis=0)
        a = -jnp.maximum(m_in, cg)
        rows_ref[4 * d + 0] = a
        rows_ref[4 * d + 1] = jnp.exp(a - b)
        rows_ref[4 * d + 2] = jnp.exp(m_in + a)
        rows_ref[4 * d + 3] = jnp.exp(w_log - m_out)
        g_ref[d] = g
        dec_ref[d] = jnp.broadcast_to(jnp.exp(btot + m_in - m_out), (nc, LANES))

    def columns(c, _):
        for d in range(2):
            x8 = jnp.concatenate([rows_ref[4 * d + k, pl.ds(c, 1), :] for k in range(4)]
                                 + [zpad[0:SUBLANES - 4, :]], axis=0)
            xt = jnp.concatenate([x8, zpad], axis=0).T
            for k in range(4):
                fac_ref[4 * d + k, c] = jnp.broadcast_to(xt[:, k:k + 1], (L, L))
        return 0

    lax.fori_loop(0, nc, columns, 0, unroll=unroll)

    def body(i, _):
        for d, h_ref in ((0, hf_ref), (1, hb_ref)):
            c = i if d == 0 else nc - 1 - i
            sl = pl.ds(pl.multiple_of(c * L, L), L)
            v_aug = jnp.concatenate([v_ref[sl, :], ones], axis=1)
            h_ref[sl, :] = _mlstm_chunk(
                q_ref[sl, :], k_ref[sl, :], v_aug, fac_ref[4 * d + 0, c], fac_ref[4 * d + 1, c],
                fac_ref[4 * d + 2, c], fac_ref[4 * d + 3, c], g_ref[d, pl.ds(c, 1), :], dec_ref[d, pl.ds(c, 1), :],
                masks[d], ct_ref.at[d], nb_ref.at[d])
        return 0

    lax.fori_loop(0, nc, body, 0, unroll=unroll)
    for d in range(2):
        nt_ref[d] = nb_ref[d].T[0:1, :]

    gmh = gmh_ref[...]

    def fin(c, _):
        sl = pl.ds(pl.multiple_of(c * L, L), L)
        hm = hf_ref[sl, :] + hb_ref[sl, :]
        mu = jnp.mean(hm, axis=-1, keepdims=True)
        var = jnp.mean(jnp.square(hm - mu), axis=-1, keepdims=True)
        y = (hm - mu) * lax.rsqrt(var + EPS) * gmh
        hm_ref[sl, :] = (jax.nn.sigmoid(o_ref[sl, :].astype(F32)) * y).astype(BF16)
        return 0

    lax.fori_loop(0, nc, fin, 0, unroll=unroll)


def _mlstm(proj, gq, g_mh, c0, n0, m0, *, n_seq, t, row0):
    H, Dh = M_HEADS, HEAD_DIM
    b0 = row0 // t
    assert row0 % t == 0 and t % CHUNK == 0
    nc = t // CHUNK

    def pspec(part):
        return pl.BlockSpec((t, Dh), lambda b, h: (b0 + b, part * H + h))

    st_spec_c = pl.BlockSpec((None, 2, None, Dh, Dh), lambda b, h: (b, 0, h, 0, 0))
    st_spec_v = pl.BlockSpec((None, 2, None, 1, LANES), lambda b, h: (b, 0, h, 0, 0))
    return pl.pallas_call(
        functools.partial(_mlstm_kernel, nc=nc, unroll=min(nc, MLSTM_UNROLL)),
        grid=(n_seq, H),
        in_specs=[pspec(0), pspec(1), pspec(2), pspec(3),
                  pl.BlockSpec((None, 4, None, nc, CHUNK), lambda b, h: (h, 0, b, 0, 0)),
                  pl.BlockSpec((1, Dh), lambda b, h: (0, h)),
                  st_spec_c, st_spec_v, st_spec_v],
        out_specs=[
            pl.BlockSpec((t, Dh), lambda b, h: (b, h)),
            st_spec_c, st_spec_v, st_spec_v,
        ],
        out_shape=[
            jax.ShapeDtypeStruct((n_seq * t, H * Dh), BF16),
            jax.ShapeDtypeStruct((n_seq, 2, H, Dh, Dh), F32),
            jax.ShapeDtypeStruct((n_seq, 2, H, 1, LANES), F32),
            jax.ShapeDtypeStruct((n_seq, 2, H, 1, LANES), F32),
        ],
        scratch_shapes=[
            pltpu.VMEM((t, Dh), F32),
            pltpu.VMEM((t, Dh), F32),
            pltpu.VMEM((2, Dh, LANES), F32),
            pltpu.VMEM((8, nc, CHUNK), F32),
            pltpu.VMEM((2, nc, CHUNK), F32),
            pltpu.VMEM((2, nc, LANES), F32),
            pltpu.VMEM((8, nc, CHUNK, LANES), F32),
        ],
        compiler_params=_cparams(("arbitrary", "arbitrary")),
        name="mlstm",
    )(proj, proj, proj, proj, gq, g_mh, c0, n0, m0)


def _conv_kernel(ga_ref, gb_ref, w_ref, bdw_ref, lng_ref, lnb_ref, u_ref, pad_ref, conv_ref, *, t, tt):
    cw = pad_ref.shape[1]
    zeros = jnp.zeros((CONV_HALO, cw), F32)
    pad_ref[0:CONV_HALO, :] = zeros
    pad_ref[CONV_HALO + t:2 * CONV_HALO + t, :] = zeros

    def fill(i, _):
        sl = pl.ds(pl.multiple_of(i * tt, tt), tt)
        glu = ga_ref[sl, :].astype(F32) * jax.nn.sigmoid(gb_ref[sl, :].astype(F32))
        pad_ref[pl.ds(pl.multiple_of(CONV_HALO + i * tt, 8), tt), :] = glu
        return 0

    lax.fori_loop(0, t // tt, fill, 0)
    shift0 = CONV_HALO - CONV_W // 2
    rows = tt + 2 * CONV_HALO

    for lg in range(cw // LANES):
        ls = slice(lg * LANES, (lg + 1) * LANES)

        def conv(i, _, ls=ls):
            t0 = pl.multiple_of(i * tt, tt)
            win = pad_ref[pl.ds(t0, rows), ls]
            acc = jnp.broadcast_to(bdw_ref[:, ls], (tt, LANES))
            for r in range(8):
                wr = win if r == 0 else pltpu.roll(win, rows - r, axis=0)
                for a in range(4):
                    j = 8 * a + r - shift0
                    if 0 <= j < CONV_W:
                        acc = acc + wr[8 * a:8 * a + tt, :] * w_ref[j:j + 1, ls]
            conv_ref[pl.ds(t0, tt), ls] = acc
            return 0

        lax.fori_loop(0, t // tt, conv, 0)

    def norm(i, _):
        t0 = pl.multiple_of(i * tt, tt)
        acc = conv_ref[pl.ds(t0, tt), :]
        mu = jnp.mean(acc, axis=-1, keepdims=True)
        var = jnp.mean(jnp.square(acc - mu), axis=-1, keepdims=True)
        y = (acc - mu) * lax.rsqrt(var + EPS) * lng_ref[...] + lnb_ref[...]
        u_ref[pl.ds(t0, tt), :] = _silu(y).astype(BF16)
        return 0

    lax.fori_loop(0, t // tt, norm, 0, unroll=min(4, t // tt))


def _conv(proj, w_dw, b_dw, ln_g, ln_b, *, n_seq, t, row0, cw):
    b0 = row0 // t
    assert row0 % t == 0
    ncol = proj.shape[1] // cw
    return pl.pallas_call(
        functools.partial(_conv_kernel, t=t, tt=CONV_ROWS),
        grid=(n_seq,),
        in_specs=[
            pl.BlockSpec((t, cw), lambda b: (b0 + b, ncol - 2)),
            pl.BlockSpec((t, cw), lambda b: (b0 + b, ncol - 1)),
            pl.BlockSpec((CONV_W + 1, cw), lambda b: (0, 0)),
            pl.BlockSpec((1, cw), lambda b: (0, 0)),
            pl.BlockSpec((1, cw), lambda b: (0, 0)),
            pl.BlockSpec((1, cw), lambda b: (0, 0)),
        ],
        out_specs=pl.BlockSpec((t, cw), lambda b: (b, 0)),
        out_shape=jax.ShapeDtypeStruct((n_seq * t, cw), BF16),
        scratch_shapes=[pltpu.VMEM((t + 2 * CONV_HALO, cw), F32), pltpu.VMEM((t, cw), F32)],
        compiler_params=_cparams(("arbitrary",)),
        name="conv",
    )(proj, proj, w_dw, b_dw, ln_g, ln_b)


def _outproj_kernel(hmp_ref, up_ref, hms_ref, us_ref, x_ref, mod_ref, w_ref, o_ref, *, d, n_prompt_tiles):
    mw = hmp_ref.shape[1]

    def project(hm_ref, u_ref):
        acc = jnp.dot(hm_ref[...], w_ref[0:mw, :], preferred_element_type=F32)
        acc = acc + jnp.dot(u_ref[...], w_ref[mw:, :], preferred_element_type=F32)
        o_ref[...] = x_ref[...] + mod_ref[:, 2 * d:3 * d] * acc

    is_prompt = pl.program_id(0) < n_prompt_tiles
    pl.when(is_prompt)(lambda: project(hmp_ref, up_ref))
    pl.when(jnp.logical_not(is_prompt))(lambda: project(hms_ref, us_ref))


def _outproj(hm_p, u_p, hm_s, u_s, x, mod, w_out, tm, row_of_tile):
    N, D = x.shape
    mw, cw = hm_p.shape[1], u_p.shape[1]
    npt = hm_p.shape[0] // tm
    return pl.pallas_call(
        functools.partial(_outproj_kernel, d=D, n_prompt_tiles=npt),
        grid=(N // tm,),
        in_specs=[
            pl.BlockSpec((tm, mw), lambda i: (jnp.minimum(i, npt - 1), 0)),
            pl.BlockSpec((tm, cw), lambda i: (jnp.minimum(i, npt - 1), 0)),
            pl.BlockSpec((tm, mw), lambda i: (jnp.maximum(i - npt, 0), 0)),
            pl.BlockSpec((tm, cw), lambda i: (jnp.maximum(i - npt, 0), 0)),
            pl.BlockSpec((tm, D), lambda i: (i, 0)),
            pl.BlockSpec((None, 1, 6 * D), lambda i: (row_of_tile(i, tm), 0, 0)),
            pl.BlockSpec((mw + cw, D), lambda i: (0, 0)),
        ],
        out_specs=pl.BlockSpec((tm, D), lambda i: (i, 0)),
        out_shape=jax.ShapeDtypeStruct((N, D), F32),
        compiler_params=_cparams(("arbitrary",)),
        name="outproj",
    )(hm_p, u_p, hm_s, u_s, x, mod, w_out)


def _ffn_kernel(x_ref, mod_ref, g_ref, wg_ref, wu_ref, wd_ref, o_ref, h_scr, acc_scr, *, d):
    j = pl.program_id(1)

    @pl.when(j == 0)
    def _():
        h_scr[...] = _rms_mod(x_ref[...], g_ref[...], mod_ref[:, 4 * d:5 * d], mod_ref[:, 3 * d:4 * d]).astype(BF16)
        acc_scr[...] = jnp.zeros_like(acc_scr)

    h = h_scr[...]
    a = jnp.dot(h, wg_ref[...], preferred_element_type=F32)
    b = jnp.dot(h, wu_ref[...], preferred_element_type=F32)
    acc_scr[...] += jnp.dot((_silu(a) * b).astype(BF16), wd_ref[...], preferred_element_type=F32)

    @pl.when(j == pl.num_programs(1) - 1)
    def _():
        o_ref[...] = x_ref[...] + mod_ref[:, 5 * d:6 * d] * acc_scr[...]


def _ffn(x, mod, g, wg, wu, wd, tm, tf, row_of_tile):
    N, D = x.shape
    ff = wg.shape[1]
    return pl.pallas_call(
        functools.partial(_ffn_kernel, d=D),
        grid=(N // tm, ff // tf),
        in_specs=[
            pl.BlockSpec((tm, D), lambda i, j: (i, 0)),
            pl.BlockSpec((None, 1, 6 * D), lambda i, j: (row_of_tile(i, tm), 0, 0)),
            pl.BlockSpec((1, D), lambda i, j: (0, 0)),
            pl.BlockSpec((D, tf), lambda i, j: (0, j)),
            pl.BlockSpec((D, tf), lambda i, j: (0, j)),
            pl.BlockSpec((tf, D), lambda i, j: (j, 0)),
        ],
        out_specs=pl.BlockSpec((tm, D), lambda i, j: (i, 0)),
        out_shape=jax.ShapeDtypeStruct((N, D), F32),
        scratch_shapes=[pltpu.VMEM((tm, D), BF16), pltpu.VMEM((tm, D), F32)],
        compiler_params=_cparams(("arbitrary", "arbitrary")),
        name="ffn",
    )(x, mod, g, wg, wu, wd)


def _router_kernel(x_ref, mod_ref, g_ref, wr_ref, h_ref, route_ref, cnt_ref, carry_ref, carry, *, d):
    i = pl.program_id(0)

    @pl.when(i == 0)
    def _():
        carry[...] = jnp.zeros_like(carry)

    h = _rms_mod(x_ref[...], g_ref[...], mod_ref[:, 4 * d:5 * d], mod_ref[:, 3 * d:4 * d])
    for k in range(d // LANES):
        h_ref[:, k, :] = h[:, k * LANES:(k + 1) * LANES]
    tm = h.shape[0]
    logits = jnp.dot(h.astype(BF16), wr_ref[...], preferred_element_type=F32)
    lane = lax.broadcasted_iota(jnp.int32, (tm, LANES), 1)
    lg = jnp.where(lane < N_EXPERTS, logits, -jnp.inf)
    m1 = jnp.max(lg, axis=1, keepdims=True)
    i1 = jnp.min(jnp.where(lg == m1, lane, LANES), axis=1, keepdims=True)
    lg2 = jnp.where(lane == i1, -jnp.inf, lg)
    m2 = jnp.max(lg2, axis=1, keepdims=True)
    i2 = jnp.min(jnp.where(lg2 == m2, lane, LANES), axis=1, keepdims=True)
    e = jnp.exp(m2 - m1)
    w1 = 1.0 / (1.0 + e)
    w2 = e / (1.0 + e)
    sel1 = lane == i1
    sel2 = lane == i2
    onehot = (sel1 | sel2).astype(BF16)
    rr = lax.broadcasted_iota(jnp.int32, (tm, tm), 0)
    cc = lax.broadcasted_iota(jnp.int32, (tm, tm), 1)
    before = (cc < rr).astype(BF16)
    carry_ref[...] = jnp.broadcast_to(carry[...], carry_ref.shape)
    rank = jnp.dot(before, onehot, preferred_element_type=F32) + carry[...]
    r1 = jnp.sum(jnp.where(sel1, rank, 0.0), axis=1, keepdims=True)
    r2 = jnp.sum(jnp.where(sel2, rank, 0.0), axis=1, keepdims=True)
    carry[...] += jnp.sum(onehot.astype(F32), axis=0, keepdims=True)
    vals = (i1.astype(F32), i2.astype(F32), w1, w2, r1, r2)
    out = jnp.zeros((tm, LANES), F32)
    for k, v in enumerate(vals):
        out = jnp.where(lane == k, v, out)
    route_ref[...] = out
    cnt_ref[...] = jnp.broadcast_to(carry[...], cnt_ref.shape)


def _router(x, mod, g, w_router, tm, row_of_tile):
    N, D = x.shape
    assert D == SUBLANES * LANES
    return pl.pallas_call(
        functools.partial(_router_kernel, d=D),
        grid=(N // tm,),
        in_specs=[
            pl.BlockSpec((tm, D), lambda i: (i, 0)),
            pl.BlockSpec((None, 1, 6 * D), lambda i: (row_of_tile(i, tm), 0, 0)),
            pl.BlockSpec((1, D), lambda i: (0, 0)),
            pl.BlockSpec((D, LANES), lambda i: (0, 0)),
        ],
        out_specs=[
            pl.BlockSpec((tm, SUBLANES, LANES), lambda i: (i, 0, 0)),
            pl.BlockSpec((tm, LANES), lambda i: (i, 0)),
            pl.BlockSpec((8, LANES), lambda i: (0, 0)),
            pl.BlockSpec((None, 8, LANES), lambda i: (i, 0, 0)),
        ],
        out_shape=[
            jax.ShapeDtypeStruct((N, SUBLANES, LANES), F32),
            jax.ShapeDtypeStruct((N, LANES), F32),
            jax.ShapeDtypeStruct((8, LANES), F32),
            jax.ShapeDtypeStruct((N // tm, 8, LANES), F32),
        ],
        scratch_shapes=[pltpu.VMEM((1, LANES), F32)],
        compiler_params=_cparams(("arbitrary",)),
        name="router",
    )(x, mod, g, w_router)


def _moe_kernel(te_ref, src0_ref, srcn_ref, h_ref, wg_ref, wu_ref, wd_ref, o_ref,
                xg, xb_scr, t_scr, acc_scr, sem, *, tm, nj):
    del te_ref
    r = pl.program_id(0)
    j = pl.program_id(1)
    slot = r % 2
    other = 1 - slot
    ch = tm // nj

    def row_copy(idx_ref, row, s):
        return pltpu.make_async_copy(h_ref.at[idx_ref[row]], xg.at[s, row], sem.at[s])

    def drain(s):
        def wait_one(t, c):
            row_copy(src0_ref, 0, s).wait()
            return c
        lax.fori_loop(0, tm, wait_one, 0)

    @pl.when((r == 0) & (j == 0))
    def _():
        def issue(t, c):
            row_copy(src0_ref, t, 0).start()
            return c
        lax.fori_loop(0, tm, issue, 0)

    @pl.when(j == 0)
    def _():
        drain(slot)
        for k in range(xg.shape[2]):
            xb_scr[:, k * LANES:(k + 1) * LANES] = xg[slot, :, k, :].astype(BF16)
        acc_scr[...] = jnp.zeros_like(acc_scr)

    base = j * ch
    h = xb_scr[...]
    tf = wg_ref.shape[1]
    n_blk = tf // MOE_COL_BLOCK
    per_blk = -(-ch // n_blk)
    for c in range(n_blk):
        for t in range(c * per_blk, min((c + 1) * per_blk, ch)):
            row_copy(srcn_ref, base + t, other).start()
        cs = slice(c * MOE_COL_BLOCK, (c + 1) * MOE_COL_BLOCK)
        a = jnp.dot(h, wg_ref[:, cs], preferred_element_type=F32)
        b = jnp.dot(h, wu_ref[:, cs], preferred_element_type=F32)
        t_scr[:, cs] = (_silu(a) * b).astype(BF16)
    acc_scr[...] += jnp.dot(t_scr[...], wd_ref[...], preferred_element_type=F32)

    @pl.when(j == nj - 1)
    def _():
        o_ref[...] = acc_scr[...]

    @pl.when((r == pl.num_programs(0) - 1) & (j == nj - 1))
    def _():
        drain(other)


def _moe(h, src, tile_expert, wg, wu, wd, tm, tf):
    P = src.shape[0]
    D = h.shape[1] * h.shape[2]
    ff = wg.shape[2]
    nj = ff // tf
    nr = P // tm
    assert tm % nj == 0 and tf % MOE_COL_BLOCK == 0
    grid_spec = pltpu.PrefetchScalarGridSpec(
        num_scalar_prefetch=1,
        grid=(nr, nj),
        in_specs=[
            pl.BlockSpec((tm,), lambda r, j, te: (0,), memory_space=pltpu.SMEM),
            pl.BlockSpec((tm,), lambda r, j, te: (jnp.minimum(r + 1, nr - 1),), memory_space=pltpu.SMEM),
            pl.BlockSpec(memory_space=pl.ANY),
            pl.BlockSpec((None, D, tf), lambda r, j, te: (te[r], 0, j)),
            pl.BlockSpec((None, D, tf), lambda r, j, te: (te[r], 0, j)),
            pl.BlockSpec((None, tf, D), lambda r, j, te: (te[r], j, 0)),
        ],
        out_specs=pl.BlockSpec((tm, D), lambda r, j, te: (r, 0)),
        scratch_shapes=[
            pltpu.VMEM((2, tm, D // LANES, LANES), F32),
            pltpu.VMEM((tm, D), BF16),
            pltpu.VMEM((tm, tf), BF16),
            pltpu.VMEM((tm, D), F32),
            pltpu.SemaphoreType.DMA((2,)),
        ],
    )
    return pl.pallas_call(
        functools.partial(_moe_kernel, tm=tm, nj=nj),
        grid_spec=grid_spec,
        out_shape=jax.ShapeDtypeStruct((P, D), F32),
        compiler_params=_cparams(("arbitrary", "arbitrary")),
        name="moe",
    )(tile_expert, src, src, h, wg, wu, wd)


def _combine_kernel(start_ref, base_ref, x_ref, route_ref, mod_ref, gfin_ref, ys_ref, *rest,
                    d, tb, final_norm, n_prompt_tiles):
    *o_refs, sbuf, sem = rest
    i = pl.program_id(0)
    slot = i % 2
    rows = tb + SUBLANES
    kdim = sbuf.shape[2]

    def slab_copy(tile, e, s):
        start = pl.multiple_of(start_ref[tile * N_EXPERTS + e], SUBLANES)
        return pltpu.make_async_copy(ys_ref.at[pl.ds(start, rows)], sbuf.at[s, e, pl.ds(0, rows)], sem.at[s])

    @pl.when(i == 0)
    def _():
        for s in range(2):
            for e in range(N_EXPERTS):
                sbuf[s, e, rows:, :] = jnp.zeros((kdim - rows, d), F32)
        for e in range(N_EXPERTS):
            slab_copy(0, e, 0).start()

    @pl.when(i + 1 < pl.num_programs(0))
    def _():
        for e in range(N_EXPERTS):
            slab_copy(i + 1, e, 1 - slot).start()

    for e in range(N_EXPERTS):
        slab_copy(i, e, slot).wait()

    route = route_ref[...]
    e1, e2, w1, w2, r1, r2 = (route[:, k:k + 1] for k in range(6))
    kcol = lax.broadcasted_iota(jnp.int32, (tb, kdim), 1).astype(F32)
    f = jnp.zeros((tb, d), F32)
    for e in range(N_EXPERTS):
        hit1 = e1 == float(e)
        hit2 = e2 == float(e)
        loc = jnp.where(hit1, r1, r2) - base_ref[i * N_EXPERTS + e].astype(F32)
        pick = ((hit1 | hit2) & (loc == kcol)).astype(BF16)
        ye = jnp.dot(pick, sbuf[slot, e].astype(BF16), preferred_element_type=F32)
        f = f + jnp.where(hit1, w1, jnp.where(hit2, w2, 0.0)) * ye
    y = x_ref[...] + mod_ref[:, 5 * d:6 * d] * f
    if final_norm:
        y = y * lax.rsqrt(jnp.mean(y * y, axis=-1, keepdims=True) + EPS) * gfin_ref[...]
    if n_prompt_tiles is None:
        o_refs[0][...] = y
    else:

        def write_prompt():
            o_refs[0][...] = y

        def write_latent():
            o_refs[1][...] = y

        pl.when(i < n_prompt_tiles)(write_prompt)
        pl.when(i >= n_prompt_tiles)(write_latent)


def _combine(x, route, mod, g_final, ys, start, base, tb, row_of_tile, final_norm, n_prompt=None):
    N, D = x.shape
    if n_prompt is None:
        npt = None
        out_specs = pl.BlockSpec((tb, D), lambda i, st, bs: (i, 0))
        out_shape = jax.ShapeDtypeStruct((N, D), F32)
    else:
        npt = n_prompt // tb
        out_specs = [pl.BlockSpec((tb, D), lambda i, st, bs: (jnp.minimum(i, npt - 1), 0)),
                     pl.BlockSpec((tb, D), lambda i, st, bs: (jnp.maximum(i - npt, 0), 0))]
        out_shape = [jax.ShapeDtypeStruct((n_prompt, D), F32), jax.ShapeDtypeStruct((N - n_prompt, D), F32)]
    grid_spec = pltpu.PrefetchScalarGridSpec(
        num_scalar_prefetch=2,
        grid=(N // tb,),
        in_specs=[
            pl.BlockSpec((tb, D), lambda i, st, bs: (i, 0)),
            pl.BlockSpec((tb, LANES), lambda i, st, bs: (i, 0)),
            pl.BlockSpec((None, 1, 6 * D), lambda i, st, bs: (row_of_tile(i, tb), 0, 0)),
            pl.BlockSpec((1, D), lambda i, st, bs: (0, 0)),
            pl.BlockSpec(memory_space=pl.ANY),
        ],
        out_specs=out_specs,
        scratch_shapes=[pltpu.VMEM((2, N_EXPERTS, tb + LANES, D), F32), pltpu.SemaphoreType.DMA((2,))],
    )
    return pl.pallas_call(
        functools.partial(_combine_kernel, d=D, tb=tb, final_norm=final_norm, n_prompt_tiles=npt),
        grid_spec=grid_spec,
        out_shape=out_shape,
        compiler_params=_cparams(("arbitrary",)),
        name="combine",
    )(start, base, x, route, mod, g_final, ys)


def _final_kernel(x_ref, g_ref, o_ref):
    x = x_ref[...]
    o_ref[...] = x * lax.rsqrt(jnp.mean(x * x, axis=-1, keepdims=True) + EPS) * g_ref[...]


def _final_norm(x, g, tm):
    N, D = x.shape
    return pl.pallas_call(
        _final_kernel,
        grid=(N // tm,),
        in_specs=[pl.BlockSpec((tm, D), lambda i: (i, 0)), pl.BlockSpec((1, D), lambda i: (0, 0))],
        out_specs=pl.BlockSpec((tm, D), lambda i: (i, 0)),
        out_shape=jax.ShapeDtypeStruct((N, D), F32),
        compiler_params=_cparams(("arbitrary",)),
        name="final_norm",
    )(x, g)


def _grid_sincos(t, grid_w, d):
    rows = t // grid_w
    row = jnp.repeat(jnp.arange(rows, dtype=F32), grid_w)
    col = jnp.tile(jnp.arange(grid_w, dtype=F32), rows)
    quarter = d // 4
    omega = 1.0 / (10000.0 ** (jnp.arange(quarter, dtype=F32) / quarter))

    def emb(p):
        a = p[:, None] * omega[None, :]
        return jnp.concatenate([jnp.sin(a), jnp.cos(a)], axis=-1)

    return jnp.concatenate([emb(row), emb(col)], axis=-1)


def _tile_plan(t_prompt_total, t_seq):
    tm = 512
    while t_prompt_total % tm or t_seq % tm:
        tm //= 2
    return tm


def kernel(x_prompt, x_sample, state_C, state_n, state_m, c, c_ctx, w_ada, b_ada, g_mix, w_in, b_gate, g_mh,
           w_dw, b_dw, ln_g, ln_b, w_out, g_ffn, w_ff_gate, w_ff_up, w_ff_down, w_router, w_moe_gate, w_moe_up,
           w_moe_down, g_final):
    B, S, D = x_prompt.shape
    DB, DS, _ = x_sample.shape
    depth = w_ada.shape[0]
    H, Dh = M_HEADS, HEAD_DIM
    mw = H * Dh
    cw = w_dw.shape[2]
    n_gates = 4 * H
    NP, NS = B * S, DB * DS
    N = NP + NS
    grid_w = 64
    assert NP % DS == 0 or DS % NP == 0
    tm = _tile_plan(NP, DS)

    def row_of_tile(i, tile):
        start = i * tile
        return jnp.where(start < NP, 0, 1 + (start - NP) // DS)

    xs = x_sample + _grid_sincos(DS, grid_w, D)[None]
    x = jnp.concatenate([x_prompt.reshape(NP, D), xs.reshape(NS, D)], axis=0)

    n_cond = 1 + DB
    r_pad = -(-n_cond // 8) * 8
    cond = jnp.zeros((r_pad, D), F32).at[0].set(c_ctx).at[1:n_cond].set(c)
    mods = _ada(cond, w_ada, b_ada).reshape(depth, r_pad, 1, 6 * D)

    zero_c = jnp.zeros((B, 2, H, Dh, Dh), F32)
    zero_n = jnp.zeros((B, 2, H, 1, LANES), F32)
    neg_m = jnp.full((B, 2, H, 1, LANES), NEG, F32)

    new_c, new_n, new_m = [], [], []
    for l in range(depth):
        mod = mods[l]
        w_l = w_in[l]
        w_main = w_l[:, :4 * mw + 2 * cw].astype(BF16)
        w_gate = jnp.pad(w_l[:, 4 * mw + 2 * cw:], ((0, 0), (0, LANES - n_gates))).astype(BF16)
        bg = jnp.pad(b_gate[l], (0, LANES - n_gates)).reshape(1, LANES)
        proj, gates = _inproj(x, mod, g_mix[l].reshape(1, D), w_main, w_gate, bg, tm, row_of_tile)

        gq = gates[:, :n_gates].reshape(N, 4, H).transpose(2, 1, 0)
        gq_p = gq[:, :, :NP].reshape(H, 4, B, S // CHUNK, CHUNK)
        gq_s = gq[:, :, NP:].reshape(H, 4, DB, DS // CHUNK, CHUNK)

        gmh = g_mh[l].reshape(1, mw)
        hm_p, c_p, n_p, m_p = _mlstm(proj, gq_p, gmh, zero_c, zero_n, neg_m, n_seq=B, t=S, row0=0)
        c_s = state_C[:, l]
        n_s = jnp.broadcast_to(state_n[:, l][:, :, :, None, :], (DB, 2, H, 1, Dh))
        m_s = jnp.broadcast_to(state_m[:, l][:, :, :, None, None], (DB, 2, H, 1, LANES))
        hm_s, _, _, _ = _mlstm(proj, gq_s, gmh, c_s, n_s, m_s, n_seq=DB, t=DS, row0=NP)
        new_c.append(c_p)
        new_n.append(n_p[:, :, :, 0, :])
        new_m.append(m_p[:, :, :, 0, 0])

        wdw = jnp.pad(w_dw[l], ((0, 1), (0, 0)))
        conv_args = (wdw, b_dw[l].reshape(1, cw), ln_g[l].reshape(1, cw), ln_b[l].reshape(1, cw))
        u_p = _conv(proj, *conv_args, n_seq=B, t=S, row0=0, cw=cw)
        u_s = _conv(proj, *conv_args, n_seq=DB, t=DS, row0=NP, cw=cw)

        x = _outproj(hm_p, u_p, hm_s, u_s, x, mod, w_out[l].astype(BF16), tm, row_of_tile)

        gf = g_ffn[l].reshape(1, D)
        if l % 2 == 0:
            e = l // 2
            ff = w_ff_gate.shape[2]
            tf = ff // 2 if (ff // 2) % LANES == 0 else ff
            x = _ffn(x, mod, gf, w_ff_gate[e].astype(BF16), w_ff_up[e].astype(BF16),
                     w_ff_down[e].astype(BF16), tm, tf, row_of_tile)
        else:
            e = l // 2
            wr = jnp.pad(w_router[e], ((0, 0), (0, LANES - N_EXPERTS))).astype(BF16)
            tb = min(tm, COMBINE_ROWS)
            h, route, cnt, carry = _router(x, mod, gf, wr, tb, row_of_tile)
            tme = tm
            counts = cnt[0, :N_EXPERTS].astype(jnp.int32)
            padded = ((counts + tme - 1) // tme) * tme
            ends = jnp.cumsum(padded)
            offs = ends - padded
            e1 = route[:, 0].astype(jnp.int32)
            e2 = route[:, 1].astype(jnp.int32)
            pos1 = offs[e1] + route[:, 4].astype(jnp.int32)
            pos2 = offs[e2] + route[:, 5].astype(jnp.int32)
            n_tiles = (2 * N) // tme + N_EXPERTS
            n_rows = n_tiles * tme
            tok = jnp.arange(N, dtype=jnp.int32)
            src = jnp.zeros((n_rows,), jnp.int32).at[jnp.concatenate([pos1, pos2])].set(
                jnp.concatenate([tok, tok]), unique_indices=True)
            starts = jnp.arange(n_tiles, dtype=jnp.int32) * tme
            tile_expert = jnp.minimum(jnp.sum(starts[:, None] >= ends[None, :], axis=1), N_EXPERTS - 1)
            ffe = w_moe_gate.shape[3]
            tfe = ffe // 2 if (ffe // 2) % LANES == 0 else ffe
            ys = _moe(h, src, tile_expert.astype(jnp.int32), w_moe_gate[e].astype(BF16),
                      w_moe_up[e].astype(BF16), w_moe_down[e].astype(BF16), tme, tfe)
            first = offs[None, :] + carry[:, 0, :N_EXPERTS].astype(jnp.int32)
            start = jnp.minimum((first // SUBLANES) * SUBLANES, n_rows - (tb + SUBLANES))
            base = start - offs[None, :]
            last = l == depth - 1
            x = _combine(x, route, mod, g_final.reshape(1, D), ys, start.reshape(-1), base.reshape(-1), tb,
                         row_of_tile, final_norm=last, n_prompt=NP if last else None)

    if (depth - 1) % 2 == 0:
        x = _final_norm(x, g_final.reshape(1, D), tm)
        x = (x[:NP], x[NP:])

    y_prompt = x[0].reshape(B, S, D)
    y_sample = x[1].reshape(DB, DS, D)
    new_state_c = jnp.stack(new_c, axis=1)
    new_state_n = jnp.stack(new_n, axis=1)
    new_state_m = jnp.stack(new_m, axis=1)
    return (y_prompt, y_sample, new_state_c, new_state_n, new_state_m)
```

```python
import functools

import jax
import jax.numpy as jnp
from jax import lax
from jax.experimental import pallas as pl
from jax.experimental.pallas import tpu as pltpu

F32 = jnp.float32
BF16 = jnp.bfloat16
EPS = 1e-6
NEG = -1e30

M_HEADS = 4
HEAD_DIM = 128
CHUNK = 128
CONV_W = 31
CONV_HALO = 16
MLSTM_UNROLL = 4
CONV_ROWS = 64
COMBINE_ROWS = 256
MOE_COL_BLOCK = 256
MOE_ROWS = 1024
N_EXPERTS = 8
LANES = 128
SUBLANES = 8
VMEM_LIMIT = 56 * 1024 * 1024


def _cparams(sem):
    return pltpu.CompilerParams(dimension_semantics=sem, vmem_limit_bytes=VMEM_LIMIT)


def _silu(a):
    return a * jax.nn.sigmoid(a)


def _rms_mod(x, g, scale, shift):
    y = x * lax.rsqrt(jnp.mean(x * x, axis=-1, keepdims=True) + EPS) * g
    return y * (1.0 + scale) + shift


def _ada_kernel(c_ref, w_ref, b_ref, o_ref):
    s = _silu(c_ref[...]).astype(BF16)
    o_ref[...] = jnp.dot(s, w_ref[...].astype(BF16), preferred_element_type=F32) + b_ref[...]


def _ada(cond, w_ada, b_ada):
    R, D = cond.shape
    L, _, D6 = w_ada.shape
    tn = 1536
    return pl.pallas_call(
        _ada_kernel,
        grid=(L, D6 // tn),
        in_specs=[
            pl.BlockSpec((R, D), lambda l, j: (0, 0)),
            pl.BlockSpec((None, D, tn), lambda l, j: (l, 0, j)),
            pl.BlockSpec((None, 1, tn), lambda l, j: (l, 0, j)),
        ],
        out_specs=pl.BlockSpec((None, R, tn), lambda l, j: (l, 0, j)),
        out_shape=jax.ShapeDtypeStruct((L, R, D6), F32),
        compiler_params=_cparams(("arbitrary", "arbitrary")),
        name="ada",
    )(cond, w_ada, b_ada.reshape(L, 1, D6))


def _inproj_kernel(x_ref, mod_ref, g_ref, w_ref, wg_ref, bg_ref, proj_ref, gates_ref, *, d, k_scale):
    h = _rms_mod(x_ref[...], g_ref[...], mod_ref[:, d:2 * d], mod_ref[:, 0:d]).astype(BF16)
    nw = w_ref.shape[1]
    cw = 512
    for j in range(nw // cw):
        r = jnp.dot(h, w_ref[:, j * cw:(j + 1) * cw], preferred_element_type=F32)
        if j == 1:
            r = r * k_scale
        proj_ref[:, j * cw:(j + 1) * cw] = r.astype(BF16)
    gates_ref[...] = jnp.dot(h, wg_ref[...], preferred_element_type=F32) + bg_ref[...]


def _inproj(x, mod, g, w_main, w_gate, b_gate, tm, row_of_tile):
    N, D = x.shape
    nw = w_main.shape[1]
    return pl.pallas_call(
        functools.partial(_inproj_kernel, d=D, k_scale=HEAD_DIM ** -0.5),
        grid=(N // tm,),
        in_specs=[
            pl.BlockSpec((tm, D), lambda i: (i, 0)),
            pl.BlockSpec((None, 1, 6 * D), lambda i: (row_of_tile(i, tm), 0, 0)),
            pl.BlockSpec((1, D), lambda i: (0, 0)),
            pl.BlockSpec((D, nw), lambda i: (0, 0)),
            pl.BlockSpec((D, LANES), lambda i: (0, 0)),
            pl.BlockSpec((1, LANES), lambda i: (0, 0)),
        ],
        out_specs=[
            pl.BlockSpec((tm, nw), lambda i: (i, 0)),
            pl.BlockSpec((tm, LANES), lambda i: (i, 0)),
        ],
        out_shape=[
            jax.ShapeDtypeStruct((N, nw), BF16),
            jax.ShapeDtypeStruct((N, LANES), F32),
        ],
        compiler_params=_cparams(("arbitrary",)),
        name="inproj",
    )(x, mod, g, w_main, w_gate, b_gate)


def _cummax_lanes(x, reverse):
    n = x.shape[1]
    lane = lax.broadcasted_iota(jnp.int32, x.shape, 1)
    sh = 1
    while sh < n:
        if reverse:
            x = jnp.maximum(x, jnp.where(lane < n - sh, pltpu.roll(x, n - sh, axis=1), -jnp.inf))
        else:
            x = jnp.maximum(x, jnp.where(lane >= sh, pltpu.roll(x, sh, axis=1), -jnp.inf))
        sh *= 2
    return x


def _mlstm_chunk(qc, kc, v_aug, a_b, e_b, wi_b, ws_b, g_row, dec, mask, c_ref, n_ref):
    L = CHUNK
    p = jnp.exp(jnp.where(mask, a_b + g_row, NEG))
    s = lax.dot_general(qc, kc, (((1,), (1,)), ((), ())), preferred_element_type=F32) * p
    r_intra = jnp.dot(s.astype(BF16), v_aug, preferred_element_type=F32)
    cn = jnp.concatenate([c_ref[...], n_ref[...]], axis=1)
    r_inter = jnp.dot(qc, cn.astype(BF16), preferred_element_type=F32)
    num = r_intra[:, :L] + wi_b * r_inter[:, :L]
    qn = r_intra[:, L:] + wi_b * r_inter[:, L:]
    kw = (kc.astype(F32) * ws_b).astype(BF16)
    r_state = lax.dot_general(kw, v_aug, (((0,), (0,)), ((), ())), preferred_element_type=F32)
    c_ref[...] = dec * cn[:, :L] + r_state[:, :L]
    n_ref[...] = dec * cn[:, L:] + r_state[:, L:]
    return num / jnp.maximum(jnp.abs(qn), e_b)


def _mlstm_kernel(q_ref, k_ref, v_ref, o_ref, gq_ref, gmh_ref, c0_ref, n0_ref, m0_ref,
                  hm_ref, ct_ref, nt_ref, mt_ref, hf_ref, hb_ref, nb_ref, rows_ref, g_ref, dec_ref, fac_ref,
                  *, nc, unroll):
    L = CHUNK
    row = lax.broadcasted_iota(jnp.int32, (L, L), 0)
    col = lax.broadcasted_iota(jnp.int32, (L, L), 1)
    masks = (col <= row, col >= row)
    zpad = jnp.zeros((L - SUBLANES, L), F32)
    ones = jnp.ones((L, L), BF16)

    ct_ref[...] = c0_ref[...]
    for d in range(2):
        nb_ref[d] = jnp.broadcast_to(n0_ref[d], (L, L)).T
        gi = gq_ref[2 * d]
        lf = jax.nn.log_sigmoid(gq_ref[2 * d + 1])
        cum = masks[1 - d].astype(F32)
        b = jnp.dot(lf, cum, preferred_element_type=F32, precision=lax.Precision.HIGHEST)
        btot = jnp.sum(lf, axis=1, keepdims=True)
        g = gi - b
        cg = _cummax_lanes(g, d == 1)
        w_log = btot - b + gi
        w_max = jnp.max(w_log, axis=1, keepdims=True)
        m = m0_ref[d][:, 0:1]
        m_in, m_out = [None] * nc, [None] * nc
        for c in (range(nc) if d == 0 else reversed(range(nc))):
            m_in[c] = m
            m = jnp.maximum(btot[c:c + 1, :] + m, w_max[c:c + 1, :])
            m_out[c] = m
        mt_ref[d] = jnp.broadcast_to(m, (1, LANES))
        m_in = jnp.concatenate(m_in, axis=0)
        m_out = jnp.concatenate(m_out, axis=0)
        a = -jnp.maximum(m_in, cg)
        rows_ref[4 * d + 0] = a
        rows_ref[4 * d + 1] = jnp.exp(a - b)
        rows_ref[4 * d + 2] = jnp.exp(m_in + a)
        rows_ref[4 * d + 3] = jnp.exp(w_log - m_out)
        g_ref[d] = g
        dec_ref[d] = jnp.broadcast_to(jnp.exp(btot + m_in - m_out), (nc, LANES))

    def columns(c, _):
        for d in range(2):
            x8 = jnp.concatenate([rows_ref[4 * d + k, pl.ds(c, 1), :] for k in range(4)]
                                 + [zpad[0:SUBLANES - 4, :]], axis=0)
            xt = jnp.concatenate([x8, zpad], axis=0).T
            for k in range(4):
                fac_ref[4 * d + k, c] = jnp.broadcast_to(xt[:, k:k + 1], (L, L))
        return 0

    lax.fori_loop(0, nc, columns, 0, unroll=unroll)

    def body(i, _):
        for d, h_ref in ((0, hf_ref), (1, hb_ref)):
            c = i if d == 0 else nc - 1 - i
            sl = pl.ds(pl.multiple_of(c * L, L), L)
            v_aug = jnp.concatenate([v_ref[sl, :], ones], axis=1)
            h_ref[sl, :] = _mlstm_chunk(
                q_ref[sl, :], k_ref[sl, :], v_aug, fac_ref[4 * d + 0, c], fac_ref[4 * d + 1, c],
                fac_ref[4 * d + 2, c], fac_ref[4 * d + 3, c], g_ref[d, pl.ds(c, 1), :], dec_ref[d, pl.ds(c, 1), :],
                masks[d], ct_ref.at[d], nb_ref.at[d])
        return 0

    lax.fori_loop(0, nc, body, 0, unroll=unroll)
    for d in range(2):
        nt_ref[d] = nb_ref[d].T[0:1, :]

    gmh = gmh_ref[...]

    def fin(c, _):
        sl = pl.ds(pl.multiple_of(c * L, L), L)
        hm = hf_ref[sl, :] + hb_ref[sl, :]
        mu = jnp.mean(hm, axis=-1, keepdims=True)
        var = jnp.mean(jnp.square(hm - mu), axis=-1, keepdims=True)
        y = (hm - mu) * lax.rsqrt(var + EPS) * gmh
        hm_ref[sl, :] = (jax.nn.sigmoid(o_ref[sl, :].astype(F32)) * y).astype(BF16)
        return 0

    lax.fori_loop(0, nc, fin, 0, unroll=unroll)


def _mlstm(proj, gq, g_mh, c0, n0, m0, *, n_seq, t, row0):
    H, Dh = M_HEADS, HEAD_DIM
    b0 = row0 // t
    assert row0 % t == 0 and t % CHUNK == 0
    nc = t // CHUNK

    def pspec(part):
        return pl.BlockSpec((t, Dh), lambda b, h: (b0 + b, part * H + h))

    st_spec_c = pl.BlockSpec((None, 2, None, Dh, Dh), lambda b, h: (b, 0, h, 0, 0))
    st_spec_v = pl.BlockSpec((None, 2, None, 1, LANES), lambda b, h: (b, 0, h, 0, 0))
    return pl.pallas_call(
        functools.partial(_mlstm_kernel, nc=nc, unroll=min(nc, MLSTM_UNROLL)),
        grid=(n_seq, H),
        in_specs=[pspec(0), pspec(1), pspec(2), pspec(3),
                  pl.BlockSpec((None, 4, None, nc, CHUNK), lambda b, h: (h, 0, b, 0, 0)),
                  pl.BlockSpec((1, Dh), lambda b, h: (0, h)),
                  st_spec_c, st_spec_v, st_spec_v],
        out_specs=[
            pl.BlockSpec((t, Dh), lambda b, h: (b, h)),
            st_spec_c, st_spec_v, st_spec_v,
        ],
        out_shape=[
            jax.ShapeDtypeStruct((n_seq * t, H * Dh), BF16),
            jax.ShapeDtypeStruct((n_seq, 2, H, Dh, Dh), F32),
            jax.ShapeDtypeStruct((n_seq, 2, H, 1, LANES), F32),
            jax.ShapeDtypeStruct((n_seq, 2, H, 1, LANES), F32),
        ],
        scratch_shapes=[
            pltpu.VMEM((t, Dh), F32),
            pltpu.VMEM((t, Dh), F32),
            pltpu.VMEM((2, Dh, LANES), F32),
            pltpu.VMEM((8, nc, CHUNK), F32),
            pltpu.VMEM((2, nc, CHUNK), F32),
            pltpu.VMEM((2, nc, LANES), F32),
            pltpu.VMEM((8, nc, CHUNK, LANES), F32),
        ],
        compiler_params=_cparams(("arbitrary", "arbitrary")),
        name="mlstm",
    )(proj, proj, proj, proj, gq, g_mh, c0, n0, m0)


def _conv_kernel(ga_ref, gb_ref, w_ref, bdw_ref, lng_ref, lnb_ref, u_ref, pad_ref, conv_ref, *, t, tt):
    cw = pad_ref.shape[1]
    zeros = jnp.zeros((CONV_HALO, cw), F32)
    pad_ref[0:CONV_HALO, :] = zeros
    pad_ref[CONV_HALO + t:2 * CONV_HALO + t, :] = zeros

    def fill(i, _):
        sl = pl.ds(pl.multiple_of(i * tt, tt), tt)
        glu = ga_ref[sl, :].astype(F32) * jax.nn.sigmoid(gb_ref[sl, :].astype(F32))
        pad_ref[pl.ds(pl.multiple_of(CONV_HALO + i * tt, 8), tt), :] = glu
        return 0

    lax.fori_loop(0, t // tt, fill, 0)
    shift0 = CONV_HALO - CONV_W // 2
    rows = tt + 2 * CONV_HALO

    for lg in range(cw // LANES):
        ls = slice(lg * LANES, (lg + 1) * LANES)

        def conv(i, _, ls=ls):
            t0 = pl.multiple_of(i * tt, tt)
            win = pad_ref[pl.ds(t0, rows), ls]
            acc = jnp.broadcast_to(bdw_ref[:, ls], (tt, LANES))
            for r in range(8):
                wr = win if r == 0 else pltpu.roll(win, rows - r, axis=0)
                for a in range(4):
                    j = 8 * a + r - shift0
                    if 0 <= j < CONV_W:
                        acc = acc + wr[8 * a:8 * a + tt, :] * w_ref[j:j + 1, ls]
            conv_ref[pl.ds(t0, tt), ls] = acc
            return 0

        lax.fori_loop(0, t // tt, conv, 0)

    def norm(i, _):
        t0 = pl.multiple_of(i * tt, tt)
        acc = conv_ref[pl.ds(t0, tt), :]
        mu = jnp.mean(acc, axis=-1, keepdims=True)
        var = jnp.mean(jnp.square(acc - mu), axis=-1, keepdims=True)
        y = (acc - mu) * lax.rsqrt(var + EPS) * lng_ref[...] + lnb_ref[...]
        u_ref[pl.ds(t0, tt), :] = _silu(y).astype(BF16)
        return 0

    lax.fori_loop(0, t // tt, norm, 0, unroll=min(4, t // tt))


def _conv(proj, w_dw, b_dw, ln_g, ln_b, *, n_seq, t, row0, cw):
    b0 = row0 // t
    assert row0 % t == 0
    ncol = proj.shape[1] // cw
    return pl.pallas_call(
        functools.partial(_conv_kernel, t=t, tt=CONV_ROWS),
        grid=(n_seq,),
        in_specs=[
            pl.BlockSpec((t, cw), lambda b: (b0 + b, ncol - 2)),
            pl.BlockSpec((t, cw), lambda b: (b0 + b, ncol - 1)),
            pl.BlockSpec((CONV_W + 1, cw), lambda b: (0, 0)),
            pl.BlockSpec((1, cw), lambda b: (0, 0)),
            pl.BlockSpec((1, cw), lambda b: (0, 0)),
            pl.BlockSpec((1, cw), lambda b: (0, 0)),
        ],
        out_specs=pl.BlockSpec((t, cw), lambda b: (b, 0)),
        out_shape=jax.ShapeDtypeStruct((n_seq * t, cw), BF16),
        scratch_shapes=[pltpu.VMEM((t + 2 * CONV_HALO, cw), F32), pltpu.VMEM((t, cw), F32)],
        compiler_params=_cparams(("arbitrary",)),
        name="conv",
    )(proj, proj, w_dw, b_dw, ln_g, ln_b)


def _outproj_kernel(hmp_ref, up_ref, hms_ref, us_ref, x_ref, mod_ref, w_ref, o_ref, *, d, n_prompt_tiles):
    mw = hmp_ref.shape[1]

    def project(hm_ref, u_ref):
        acc = jnp.dot(hm_ref[...], w_ref[0:mw, :], preferred_element_type=F32)
        acc = acc + jnp.dot(u_ref[...], w_ref[mw:, :], preferred_element_type=F32)
        o_ref[...] = x_ref[...] + mod_ref[:, 2 * d:3 * d] * acc

    is_prompt = pl.program_id(0) < n_prompt_tiles
    pl.when(is_prompt)(lambda: project(hmp_ref, up_ref))
    pl.when(jnp.logical_not(is_prompt))(lambda: project(hms_ref, us_ref))


def _outproj(hm_p, u_p, hm_s, u_s, x, mod, w_out, tm, row_of_tile):
    N, D = x.shape
    mw, cw = hm_p.shape[1], u_p.shape[1]
    npt = hm_p.shape[0] // tm
    return pl.pallas_call(
        functools.partial(_outproj_kernel, d=D, n_prompt_tiles=npt),
        grid=(N // tm,),
        in_specs=[
            pl.BlockSpec((tm, mw), lambda i: (jnp.minimum(i, npt - 1), 0)),
            pl.BlockSpec((tm, cw), lambda i: (jnp.minimum(i, npt - 1), 0)),
            pl.BlockSpec((tm, mw), lambda i: (jnp.maximum(i - npt, 0), 0)),
            pl.BlockSpec((tm, cw), lambda i: (jnp.maximum(i - npt, 0), 0)),
            pl.BlockSpec((tm, D), lambda i: (i, 0)),
            pl.BlockSpec((None, 1, 6 * D), lambda i: (row_of_tile(i, tm), 0, 0)),
            pl.BlockSpec((mw + cw, D), lambda i: (0, 0)),
        ],
        out_specs=pl.BlockSpec((tm, D), lambda i: (i, 0)),
        out_shape=jax.ShapeDtypeStruct((N, D), F32),
        compiler_params=_cparams(("arbitrary",)),
        name="outproj",
    )(hm_p, u_p, hm_s, u_s, x, mod, w_out)


def _ffn_kernel(x_ref, mod_ref, g_ref, wg_ref, wu_ref, wd_ref, o_ref, h_scr, acc_scr, *, d):
    j = pl.program_id(1)

    @pl.when(j == 0)
    def _():
        h_scr[...] = _rms_mod(x_ref[...], g_ref[...], mod_ref[:, 4 * d:5 * d], mod_ref[:, 3 * d:4 * d]).astype(BF16)
        acc_scr[...] = jnp.zeros_like(acc_scr)

    h = h_scr[...]
    a = jnp.dot(h, wg_ref[...], preferred_element_type=F32)
    b = jnp.dot(h, wu_ref[...], preferred_element_type=F32)
    acc_scr[...] += jnp.dot((_silu(a) * b).astype(BF16), wd_ref[...], preferred_element_type=F32)

    @pl.when(j == pl.num_programs(1) - 1)
    def _():
        o_ref[...] = x_ref[...] + mod_ref[:, 5 * d:6 * d] * acc_scr[...]


def _ffn(x, mod, g, wg, wu, wd, tm, tf, row_of_tile):
    N, D = x.shape
    ff = wg.shape[1]
    return pl.pallas_call(
        functools.partial(_ffn_kernel, d=D),
        grid=(N // tm, ff // tf),
        in_specs=[
            pl.BlockSpec((tm, D), lambda i, j: (i, 0)),
            pl.BlockSpec((None, 1, 6 * D), lambda i, j: (row_of_tile(i, tm), 0, 0)),
            pl.BlockSpec((1, D), lambda i, j: (0, 0)),
            pl.BlockSpec((D, tf), lambda i, j: (0, j)),
            pl.BlockSpec((D, tf), lambda i, j: (0, j)),
            pl.BlockSpec((tf, D), lambda i, j: (j, 0)),
        ],
        out_specs=pl.BlockSpec((tm, D), lambda i, j: (i, 0)),
        out_shape=jax.ShapeDtypeStruct((N, D), F32),
        scratch_shapes=[pltpu.VMEM((tm, D), BF16), pltpu.VMEM((tm, D), F32)],
        compiler_params=_cparams(("arbitrary", "arbitrary")),
        name="ffn",
    )(x, mod, g, wg, wu, wd)


def _router_kernel(x_ref, mod_ref, g_ref, wr_ref, h_ref, route_ref, cnt_ref, carry_ref, carry, *, d):
    i = pl.program_id(0)

    @pl.when(i == 0)
    def _():
        carry[...] = jnp.zeros_like(carry)

    h = _rms_mod(x_ref[...], g_ref[...], mod_ref[:, 4 * d:5 * d], mod_ref[:, 3 * d:4 * d])
    for k in range(d // LANES):
        h_ref[:, k, :] = h[:, k * LANES:(k + 1) * LANES]
    tm = h.shape[0]
    logits = jnp.dot(h.astype(BF16), wr_ref[...], preferred_element_type=F32)
    lane = lax.broadcasted_iota(jnp.int32, (tm, LANES), 1)
    lg = jnp.where(lane < N_EXPERTS, logits, -jnp.inf)
    m1 = jnp.max(lg, axis=1, keepdims=True)
    i1 = jnp.min(jnp.where(lg == m1, lane, LANES), axis=1, keepdims=True)
    lg2 = jnp.where(lane == i1, -jnp.inf, lg)
    m2 = jnp.max(lg2, axis=1, keepdims=True)
    i2 = jnp.min(jnp.where(lg2 == m2, lane, LANES), axis=1, keepdims=True)
    e = jnp.exp(m2 - m1)
    w1 = 1.0 / (1.0 + e)
    w2 = e / (1.0 + e)
    sel1 = lane == i1
    sel2 = lane == i2
    onehot = (sel1 | sel2).astype(BF16)
    rr = lax.broadcasted_iota(jnp.int32, (tm, tm), 0)
    cc = lax.broadcasted_iota(jnp.int32, (tm, tm), 1)
    before = (cc < rr).astype(BF16)
    carry_ref[...] = jnp.broadcast_to(carry[...], carry_ref.shape)
    rank = jnp.dot(before, onehot, preferred_element_type=F32) + carry[...]
    r1 = jnp.sum(jnp.where(sel1, rank, 0.0), axis=1, keepdims=True)
    r2 = jnp.sum(jnp.where(sel2, rank, 0.0), axis=1, keepdims=True)
    carry[...] += jnp.sum(onehot.astype(F32), axis=0, keepdims=True)
    vals = (i1.astype(F32), i2.astype(F32), w1, w2, r1, r2)
    out = jnp.zeros((tm, LANES), F32)
    for k, v in enumerate(vals):
        out = jnp.where(lane == k, v, out)
    route_ref[...] = out
    cnt_ref[...] = jnp.broadcast_to(carry[...], cnt_ref.shape)


def _router(x, mod, g, w_router, tm, row_of_tile):
    N, D = x.shape
    assert D == SUBLANES * LANES
    return pl.pallas_call(
        functools.partial(_router_kernel, d=D),
        grid=(N // tm,),
        in_specs=[
            pl.BlockSpec((tm, D), lambda i: (i, 0)),
            pl.BlockSpec((None, 1, 6 * D), lambda i: (row_of_tile(i, tm), 0, 0)),
            pl.BlockSpec((1, D), lambda i: (0, 0)),
            pl.BlockSpec((D, LANES), lambda i: (0, 0)),
        ],
        out_specs=[
            pl.BlockSpec((tm, SUBLANES, LANES), lambda i: (i, 0, 0)),
            pl.BlockSpec((tm, LANES), lambda i: (i, 0)),
            pl.BlockSpec((8, LANES), lambda i: (0, 0)),
            pl.BlockSpec((None, 8, LANES), lambda i: (i, 0, 0)),
        ],
        out_shape=[
            jax.ShapeDtypeStruct((N, SUBLANES, LANES), F32),
            jax.ShapeDtypeStruct((N, LANES), F32),
            jax.ShapeDtypeStruct((8, LANES), F32),
            jax.ShapeDtypeStruct((N // tm, 8, LANES), F32),
        ],
        scratch_shapes=[pltpu.VMEM((1, LANES), F32)],
        compiler_params=_cparams(("arbitrary",)),
        name="router",
    )(x, mod, g, w_router)


def _moe_kernel(te_ref, src0_ref, srcn_ref, h_ref, wg_ref, wu_ref, wd_ref, o_ref,
                xg, xb_scr, t_scr, acc_scr, sem, *, tm, nj):
    del te_ref
    r = pl.program_id(0)
    j = pl.program_id(1)
    slot = r % 2
    other = 1 - slot
    ch = tm // nj

    def row_copy(idx_ref, row, s):
        return pltpu.make_async_copy(h_ref.at[idx_ref[row]], xg.at[s, row], sem.at[s])

    def drain(s):
        def wait_one(t, c):
            row_copy(src0_ref, 0, s).wait()
            return c
        lax.fori_loop(0, tm, wait_one, 0)

    @pl.when((r == 0) & (j == 0))
    def _():
        def issue(t, c):
            row_copy(src0_ref, t, 0).start()
            return c
        lax.fori_loop(0, tm, issue, 0)

    @pl.when(j == 0)
    def _():
        drain(slot)
        for k in range(xg.shape[2]):
            xb_scr[:, k * LANES:(k + 1) * LANES] = xg[slot, :, k, :].astype(BF16)
        acc_scr[...] = jnp.zeros_like(acc_scr)

    base = j * ch
    h = xb_scr[...]
    tf = wg_ref.shape[1]
    n_blk = tf // MOE_COL_BLOCK
    per_blk = -(-ch // n_blk)
    for c in range(n_blk):
        for t in range(c * per_blk, min((c + 1) * per_blk, ch)):
            row_copy(srcn_ref, base + t, other).start()
        cs = slice(c * MOE_COL_BLOCK, (c + 1) * MOE_COL_BLOCK)
        a = jnp.dot(h, wg_ref[:, cs], preferred_element_type=F32)
        b = jnp.dot(h, wu_ref[:, cs], preferred_element_type=F32)
        t_scr[:, cs] = (_silu(a) * b).astype(BF16)
    acc_scr[...] += jnp.dot(t_scr[...], wd_ref[...], preferred_element_type=F32)

    @pl.when(j == nj - 1)
    def _():
        o_ref[...] = acc_scr[...]

    @pl.when((r == pl.num_programs(0) - 1) & (j == nj - 1))
    def _():
        drain(other)


def _moe(h, src, tile_expert, wg, wu, wd, tm, tf):
    P = src.shape[0]
    D = h.shape[1] * h.shape[2]
    ff = wg.shape[2]
    nj = ff // tf
    nr = P // tm
    assert tm % nj == 0 and tf % MOE_COL_BLOCK == 0
    grid_spec = pltpu.PrefetchScalarGridSpec(
        num_scalar_prefetch=1,
        grid=(nr, nj),
        in_specs=[
            pl.BlockSpec((tm,), lambda r, j, te: (0,), memory_space=pltpu.SMEM),
            pl.BlockSpec((tm,), lambda r, j, te: (jnp.minimum(r + 1, nr - 1),), memory_space=pltpu.SMEM),
            pl.BlockSpec(memory_space=pl.ANY),
            pl.BlockSpec((None, D, tf), lambda r, j, te: (te[r], 0, j)),
            pl.BlockSpec((None, D, tf), lambda r, j, te: (te[r], 0, j)),
            pl.BlockSpec((None, tf, D), lambda r, j, te: (te[r], j, 0)),
        ],
        out_specs=pl.BlockSpec((tm, D), lambda r, j, te: (r, 0)),
        scratch_shapes=[
            pltpu.VMEM((2, tm, D // LANES, LANES), F32),
            pltpu.VMEM((tm, D), BF16),
            pltpu.VMEM((tm, tf), BF16),
            pltpu.VMEM((tm, D), F32),
            pltpu.SemaphoreType.DMA((2,)),
        ],
    )
    return pl.pallas_call(
        functools.partial(_moe_kernel, tm=tm, nj=nj),
        grid_spec=grid_spec,
        out_shape=jax.ShapeDtypeStruct((P, D), F32),
        compiler_params=_cparams(("arbitrary", "arbitrary")),
        name="moe",
    )(tile_expert, src, src, h, wg, wu, wd)


def _combine_kernel(start_ref, base_ref, x_ref, route_ref, mod_ref, gfin_ref, ys_ref, *rest,
                    d, tb, final_norm, n_prompt_tiles):
    *o_refs, sbuf, sem = rest
    i = pl.program_id(0)
    slot = i % 2
    rows = tb + SUBLANES
    kdim = sbuf.shape[2]

    def slab_copy(tile, e, s):
        start = pl.multiple_of(start_ref[tile * N_EXPERTS + e], SUBLANES)
        return pltpu.make_async_copy(ys_ref.at[pl.ds(start, rows)], sbuf.at[s, e, pl.ds(0, rows)], sem.at[s])

    @pl.when(i == 0)
    def _():
        for s in range(2):
            for e in range(N_EXPERTS):
                sbuf[s, e, rows:, :] = jnp.zeros((kdim - rows, d), F32)
        for e in range(N_EXPERTS):
            slab_copy(0, e, 0).start()

    @pl.when(i + 1 < pl.num_programs(0))
    def _():
        for e in range(N_EXPERTS):
            slab_copy(i + 1, e, 1 - slot).start()

    for e in range(N_EXPERTS):
        slab_copy(i, e, slot).wait()

    route = route_ref[...]
    e1, e2, w1, w2, r1, r2 = (route[:, k:k + 1] for k in range(6))
    kcol = lax.broadcasted_iota(jnp.int32, (tb, kdim), 1).astype(F32)
    f = jnp.zeros((tb, d), F32)
    for e in range(N_EXPERTS):
        hit1 = e1 == float(e)
        hit2 = e2 == float(e)
        loc = jnp.where(hit1, r1, r2) - base_ref[i * N_EXPERTS + e].astype(F32)
        pick = ((hit1 | hit2) & (loc == kcol)).astype(BF16)
        ye = jnp.dot(pick, sbuf[slot, e].astype(BF16), preferred_element_type=F32)
        f = f + jnp.where(hit1, w1, jnp.where(hit2, w2, 0.0)) * ye
    y = x_ref[...] + mod_ref[:, 5 * d:6 * d] * f
    if final_norm:
        y = y * lax.rsqrt(jnp.mean(y * y, axis=-1, keepdims=True) + EPS) * gfin_ref[...]
    if n_prompt_tiles is None:
        o_refs[0][...] = y
    else:

        def write_prompt():
            o_refs[0][...] = y

        def write_latent():
            o_refs[1][...] = y

        pl.when(i < n_prompt_tiles)(write_prompt)
        pl.when(i >= n_prompt_tiles)(write_latent)


def _combine(x, route, mod, g_final, ys, start, base, tb, row_of_tile, final_norm, n_prompt=None):
    N, D = x.shape
    if n_prompt is None:
        npt = None
        out_specs = pl.BlockSpec((tb, D), lambda i, st, bs: (i, 0))
        out_shape = jax.ShapeDtypeStruct((N, D), F32)
    else:
        npt = n_prompt // tb
        out_specs = [pl.BlockSpec((tb, D), lambda i, st, bs: (jnp.minimum(i, npt - 1), 0)),
                     pl.BlockSpec((tb, D), lambda i, st, bs: (jnp.maximum(i - npt, 0), 0))]
        out_shape = [jax.ShapeDtypeStruct((n_prompt, D), F32), jax.ShapeDtypeStruct((N - n_prompt, D), F32)]
    grid_spec = pltpu.PrefetchScalarGridSpec(
        num_scalar_prefetch=2,
        grid=(N // tb,),
        in_specs=[
            pl.BlockSpec((tb, D), lambda i, st, bs: (i, 0)),
            pl.BlockSpec((tb, LANES), lambda i, st, bs: (i, 0)),
            pl.BlockSpec((None, 1, 6 * D), lambda i, st, bs: (row_of_tile(i, tb), 0, 0)),
            pl.BlockSpec((1, D), lambda i, st, bs: (0, 0)),
            pl.BlockSpec(memory_space=pl.ANY),
        ],
        out_specs=out_specs,
        scratch_shapes=[pltpu.VMEM((2, N_EXPERTS, tb + LANES, D), F32), pltpu.SemaphoreType.DMA((2,))],
    )
    return pl.pallas_call(
        functools.partial(_combine_kernel, d=D, tb=tb, final_norm=final_norm, n_prompt_tiles=npt),
        grid_spec=grid_spec,
        out_shape=out_shape,
        compiler_params=_cparams(("arbitrary",)),
        name="combine",
    )(start, base, x, route, mod, g_final, ys)


def _final_kernel(x_ref, g_ref, o_ref):
    x = x_ref[...]
    o_ref[...] = x * lax.rsqrt(jnp.mean(x * x, axis=-1, keepdims=True) + EPS) * g_ref[...]


def _final_norm(x, g, tm):
    N, D = x.shape
    return pl.pallas_call(
        _final_kernel,
        grid=(N // tm,),
        in_specs=[pl.BlockSpec((tm, D), lambda i: (i, 0)), pl.BlockSpec((1, D), lambda i: (0, 0))],
        out_specs=pl.BlockSpec((tm, D), lambda i: (i, 0)),
        out_shape=jax.ShapeDtypeStruct((N, D), F32),
        compiler_params=_cparams(("arbitrary",)),
        name="final_norm",
    )(x, g)


def _grid_sincos(t, grid_w, d):
    rows = t // grid_w
    row = jnp.repeat(jnp.arange(rows, dtype=F32), grid_w)
    col = jnp.tile(jnp.arange(grid_w, dtype=F32), rows)
    quarter = d // 4
    omega = 1.0 / (10000.0 ** (jnp.arange(quarter, dtype=F32) / quarter))

    def emb(p):
        a = p[:, None] * omega[None, :]
        return jnp.concatenate([jnp.sin(a), jnp.cos(a)], axis=-1)

    return jnp.concatenate([emb(row), emb(col)], axis=-1)


def _tile_plan(t_prompt_total, t_seq):
    tm = 512
    while t_prompt_total % tm or t_seq % tm:
        tm //= 2
    return tm


def kernel(x_prompt, x_sample, state_C, state_n, state_m, c, c_ctx, w_ada, b_ada, g_mix, w_in, b_gate, g_mh,
           w_dw, b_dw, ln_g, ln_b, w_out, g_ffn, w_ff_gate, w_ff_up, w_ff_down, w_router, w_moe_gate, w_moe_up,
           w_moe_down, g_final):
    B, S, D = x_prompt.shape
    DB, DS, _ = x_sample.shape
    depth = w_ada.shape[0]
    H, Dh = M_HEADS, HEAD_DIM
    mw = H * Dh
    cw = w_dw.shape[2]
    n_gates = 4 * H
    NP, NS = B * S, DB * DS
    N = NP + NS
    grid_w = 64
    assert NP % DS == 0 or DS % NP == 0
    tm = _tile_plan(NP, DS)

    def row_of_tile(i, tile):
        start = i * tile
        return jnp.where(start < NP, 0, 1 + (start - NP) // DS)

    xs = x_sample + _grid_sincos(DS, grid_w, D)[None]
    x = jnp.concatenate([x_prompt.reshape(NP, D), xs.reshape(NS, D)], axis=0)

    n_cond = 1 + DB
    r_pad = -(-n_cond // 8) * 8
    cond = jnp.zeros((r_pad, D), F32).at[0].set(c_ctx).at[1:n_cond].set(c)
    mods = _ada(cond, w_ada, b_ada).reshape(depth, r_pad, 1, 6 * D)

    zero_c = jnp.zeros((B, 2, H, Dh, Dh), F32)
    zero_n = jnp.zeros((B, 2, H, 1, LANES), F32)
    neg_m = jnp.full((B, 2, H, 1, LANES), NEG, F32)

    new_c, new_n, new_m = [], [], []
    for l in range(depth):
        mod = mods[l]
        w_l = w_in[l]
        w_main = w_l[:, :4 * mw + 2 * cw].astype(BF16)
        w_gate = jnp.pad(w_l[:, 4 * mw + 2 * cw:], ((0, 0), (0, LANES - n_gates))).astype(BF16)
        bg = jnp.pad(b_gate[l], (0, LANES - n_gates)).reshape(1, LANES)
        proj, gates = _inproj(x, mod, g_mix[l].reshape(1, D), w_main, w_gate, bg, tm, row_of_tile)

        gq = gates[:, :n_gates].reshape(N, 4, H).transpose(2, 1, 0)
        gq_p = gq[:, :, :NP].reshape(H, 4, B, S // CHUNK, CHUNK)
        gq_s = gq[:, :, NP:].reshape(H, 4, DB, DS // CHUNK, CHUNK)

        gmh = g_mh[l].reshape(1, mw)
        hm_p, c_p, n_p, m_p = _mlstm(proj, gq_p, gmh, zero_c, zero_n, neg_m, n_seq=B, t=S, row0=0)
        c_s = state_C[:, l]
        n_s = jnp.broadcast_to(state_n[:, l][:, :, :, None, :], (DB, 2, H, 1, Dh))
        m_s = jnp.broadcast_to(state_m[:, l][:, :, :, None, None], (DB, 2, H, 1, LANES))
        hm_s, _, _, _ = _mlstm(proj, gq_s, gmh, c_s, n_s, m_s, n_seq=DB, t=DS, row0=NP)
        new_c.append(c_p)
        new_n.append(n_p[:, :, :, 0, :])
        new_m.append(m_p[:, :, :, 0, 0])

        wdw = jnp.pad(w_dw[l], ((0, 1), (0, 0)))
        conv_args = (wdw, b_dw[l].reshape(1, cw), ln_g[l].reshape(1, cw), ln_b[l].reshape(1, cw))
        u_p = _conv(proj, *conv_args, n_seq=B, t=S, row0=0, cw=cw)
        u_s = _conv(proj, *conv_args, n_seq=DB, t=DS, row0=NP, cw=cw)

        x = _outproj(hm_p, u_p, hm_s, u_s, x, mod, w_out[l].astype(BF16), tm, row_of_tile)

        gf = g_ffn[l].reshape(1, D)
        if l % 2 == 0:
            e = l // 2
            ff = w_ff_gate.shape[2]
            tf = ff // 2 if (ff // 2) % LANES == 0 else ff
            x = _ffn(x, mod, gf, w_ff_gate[e].astype(BF16), w_ff_up[e].astype(BF16),
                     w_ff_down[e].astype(BF16), tm, tf, row_of_tile)
        else:
            e = l // 2
            wr = jnp.pad(w_router[e], ((0, 0), (0, LANES - N_EXPERTS))).astype(BF16)
            tb = min(tm, COMBINE_ROWS)
            h, route, cnt, carry = _router(x, mod, gf, wr, tb, row_of_tile)
            tme = MOE_ROWS if (2 * N) % MOE_ROWS == 0 else tm
            counts = cnt[0, :N_EXPERTS].astype(jnp.int32)
            padded = ((counts + tme - 1) // tme) * tme
            ends = jnp.cumsum(padded)
            offs = ends - padded
            e1 = route[:, 0].astype(jnp.int32)
            e2 = route[:, 1].astype(jnp.int32)
            pos1 = offs[e1] + route[:, 4].astype(jnp.int32)
            pos2 = offs[e2] + route[:, 5].astype(jnp.int32)
            n_tiles = (2 * N) // tme + N_EXPERTS
            n_rows = n_tiles * tme
            tok = jnp.arange(N, dtype=jnp.int32)
            src = jnp.zeros((n_rows,), jnp.int32).at[jnp.concatenate([pos1, pos2])].set(
                jnp.concatenate([tok, tok]), unique_indices=True)
            starts = jnp.arange(n_tiles, dtype=jnp.int32) * tme
            tile_expert = jnp.minimum(jnp.sum(starts[:, None] >= ends[None, :], axis=1), N_EXPERTS - 1)
            ffe = w_moe_gate.shape[3]
            tfe = ffe // 2 if (ffe // 2) % LANES == 0 else ffe
            ys = _moe(h, src, tile_expert.astype(jnp.int32), w_moe_gate[e].astype(BF16),
                      w_moe_up[e].astype(BF16), w_moe_down[e].astype(BF16), tme, tfe)
            first = offs[None, :] + carry[:, 0, :N_EXPERTS].astype(jnp.int32)
            start = jnp.minimum((first // SUBLANES) * SUBLANES, n_rows - (tb + SUBLANES))
            base = start - offs[None, :]
            last = l == depth - 1
            x = _combine(x, route, mod, g_final.reshape(1, D), ys, start.reshape(-1), base.reshape(-1), tb,
                         row_of_tile, final_norm=last, n_prompt=NP if last else None)

    if (depth - 1) % 2 == 0:
        x = _final_norm(x, g_final.reshape(1, D), tm)
        x = (x[:NP], x[NP:])

    y_prompt = x[0].reshape(B, S, D)
    y_sample = x[1].reshape(DB, DS, D)
    new_state_c = jnp.stack(new_c, axis=1)
    new_state_n = jnp.stack(new_n, axis=1)
    new_state_m = jnp.stack(new_m, axis=1)
    return (y_prompt, y_sample, new_state_c, new_state_n, new_state_m)
```
